```python
import math
import jax, jax.numpy as jnp
from jax import lax
import numpy as np

D_MODEL = 1024
BATCH = 16
SEQ = 2048
DEPTH = 1

MIX_WIDTH = D_MODEL
ATTN_WIDTH = MIX_WIDTH // 2
REC_WIDTH = MIX_WIDTH - ATTN_WIDTH
DA_HEAD_DIM = 64
DA_V_DIM = 2 * DA_HEAD_DIM
DA_N_HEADS = ATTN_WIDTH // DA_V_DIM
REC_BLOCKS = 8
REC_BLOCK_DIM = REC_WIDTH // REC_BLOCKS
CONV_WIDTH = 4
LRU_C = 8.0
D_FF = -(-8 * D_MODEL // (3 * 256)) * 256
Q_BLOCK = 128
IN_WIDTH = 3 * ATTN_WIDTH + 2 * REC_WIDTH
DEEPNORM_ALPHA = (2.0 * DEPTH) ** 0.25
DEEPNORM_BETA = (8.0 * DEPTH) ** -0.25
LN_EPS = 1e-5
RMS_EPS = 1e-5

kernel_name = "hymba_style_diffattn_rglru_deepnorm"


def layer_norm(x, g, b):
    xf = x.astype(jnp.float32)
    mu = jnp.mean(xf, axis=-1, keepdims=True)
    xc = xf - mu
    var = jnp.mean(xc * xc, axis=-1, keepdims=True)
    return (xc * lax.rsqrt(var + LN_EPS) * g.astype(jnp.float32) + b.astype(jnp.float32)).astype(x.dtype)


def rms_norm(x, g):
    xf = x.astype(jnp.float32)
    ms = jnp.mean(xf * xf, axis=-1, keepdims=True)
    return (xf * lax.rsqrt(ms + RMS_EPS) * g.astype(jnp.float32)).astype(x.dtype)


def diff_lambda(lam_params, lambda_init):
    lp = lam_params.astype(jnp.float32)
    return jnp.exp(jnp.sum(lp[0] * lp[1])) - jnp.exp(jnp.sum(lp[2] * lp[3])) + lambda_init


def differential_attention(q, k, v, lam, sub_g, lambda_init):
    B, S = q.shape[0], q.shape[1]
    q = q * (DA_HEAD_DIM ** -0.5)
    outs = []
    for blk in range(S // Q_BLOCK):
        q0 = blk * Q_BLOCK
        kend = q0 + Q_BLOCK
        qb = q[:, q0:kend]
        kb = k[:, :kend]
        vb = v[:, :kend]
        s = jnp.einsum('bqhcd,bkhcd->bhcqk', qb, kb).astype(jnp.float32)
        mask = (q0 + jnp.arange(Q_BLOCK))[:, None] >= jnp.arange(kend)[None, :]
        s = jnp.where(mask, s, -jnp.inf)
        p = jax.nn.softmax(s, axis=-1)
        w = p[:, :, 0] - lam * p[:, :, 1]
        outs.append(jnp.einsum('bhqk,bkhe->bqhe', w.astype(v.dtype), vb))
    o = jnp.concatenate(outs, axis=1)
    o = rms_norm(o, sub_g) * (1.0 - lambda_init)
    return o.reshape(B, S, DA_N_HEADS * DA_V_DIM)


def causal_depthwise_conv(x, w, b):
    S = x.shape[1]
    xp = jnp.pad(x, ((0, 0), (CONV_WIDTH - 1, 0), (0, 0)))
    out = b
    for tap in range(CONV_WIDTH):
        out = out + xp[:, tap:tap + S] * w[tap]
    return out


def rg_lru(x, w_a, b_a, w_x, b_x, lam):
    B, S, C = x.shape
    xf = x.astype(jnp.float32)
    xb = xf.reshape(B, S, REC_BLOCKS, REC_BLOCK_DIM)
    r = jax.nn.sigmoid(jnp.einsum('bsnd,nde->bsne', xb, w_a.astype(jnp.float32)).reshape(B, S, C) + b_a)
    i = jax.nn.sigmoid(jnp.einsum('bsnd,nde->bsne', xb, w_x.astype(jnp.float32)).reshape(B, S, C) + b_x)
    log_a = -LRU_C * r * jax.nn.softplus(-lam.astype(jnp.float32))
    a = jnp.exp(log_a)
    mult = jnp.sqrt(-jnp.expm1(2.0 * log_a))
    first = (jnp.arange(S) == 0)[None, :, None]
    mult = jnp.where(first, 1.0, mult)
    bterm = mult * (i * xf)

    def combine(e1, e2):
        a1, b1 = e1
        a2, b2 = e2
        return a1 * a2, a2 * b1 + b2

    _, h = lax.associative_scan(combine, (a, bterm), axis=1)
    return h.astype(x.dtype)


def hybrid_mixer(x, w_in, conv_w, conv_b, w_a, b_a, w_x, b_x, lru_lam, rec_g,
                 da_lam, da_g, w_out, lambda_init):
    B, S, _ = x.shape
    proj = jnp.einsum('bsd,de->bse', x, w_in)
    q, k, v, xr, gate = jnp.split(
        proj, [ATTN_WIDTH, 2 * ATTN_WIDTH, 3 * ATTN_WIDTH, 3 * ATTN_WIDTH + REC_WIDTH], axis=-1)
    q = q.reshape(B, S, DA_N_HEADS, 2, DA_HEAD_DIM)
    k = k.reshape(B, S, DA_N_HEADS, 2, DA_HEAD_DIM)
    v = v.reshape(B, S, DA_N_HEADS, DA_V_DIM)
    lam = diff_lambda(da_lam, lambda_init)
    attn_out = differential_attention(q, k, v, lam, da_g, lambda_init)
    xr = causal_depthwise_conv(xr, conv_w, conv_b)
    h = rg_lru(xr, w_a, b_a, w_x, b_x, lru_lam)
    rec_out = rms_norm(h * jax.nn.gelu(gate, approximate=True), rec_g)
    merged = jnp.concatenate([attn_out, rec_out], axis=-1)
    return jnp.einsum('bse,ed->bsd', merged, w_out)


def swiglu_ffn(x, w_ffn_in, w_ffn_out):
    hu = jnp.einsum('bsd,df->bsf', x, w_ffn_in)
    g, u = jnp.split(hu, 2, axis=-1)
    return jnp.einsum('bsf,fd->bsd', jax.nn.silu(g) * u, w_ffn_out)


def setup_inputs(seed: int = 0) -> dict:
    key = jax.random.key(seed)
    ks = jax.random.split(key, 20)
    f32 = jnp.float32
    x = jax.random.normal(ks[0], (BATCH, SEQ, D_MODEL), f32)
    col_scale = jnp.concatenate([
        jnp.ones((2 * ATTN_WIDTH,), f32),
        jnp.full((ATTN_WIDTH + REC_WIDTH,), DEEPNORM_BETA, f32),
        jnp.ones((REC_WIDTH,), f32)])
    w_in = jax.random.normal(ks[1], (DEPTH, D_MODEL, IN_WIDTH), f32) * (D_MODEL ** -0.5) * col_scale
    conv_w = jax.random.normal(ks[2], (DEPTH, CONV_WIDTH, REC_WIDTH), f32) * (CONV_WIDTH ** -0.5)
    conv_b = 0.01 * jax.random.normal(ks[3], (DEPTH, REC_WIDTH), f32)
    lru_w_a = jax.random.normal(ks[4], (DEPTH, REC_BLOCKS, REC_BLOCK_DIM, REC_BLOCK_DIM), f32) * (REC_BLOCK_DIM ** -0.5)
    lru_b_a = 0.01 * jax.random.normal(ks[5], (DEPTH, REC_WIDTH), f32)
    lru_w_x = jax.random.normal(ks[6], (DEPTH, REC_BLOCKS, REC_BLOCK_DIM, REC_BLOCK_DIM), f32) * (REC_BLOCK_DIM ** -0.5)
    lru_b_x = 0.01 * jax.random.normal(ks[7], (DEPTH, REC_WIDTH), f32)
    u = jax.random.uniform(ks[8], (DEPTH, REC_WIDTH), f32, 0.9, 0.999)
    s = u ** (1.0 / LRU_C)
    lru_lambda = jnp.log(s) - jnp.log1p(-s)
    rec_norm_g = 1.0 + 0.02 * jax.random.normal(ks[9], (DEPTH, REC_WIDTH), f32)
    da_lambda = 0.1 * jax.random.normal(ks[10], (DEPTH, 4, DA_HEAD_DIM), f32)
    da_norm_g = 1.0 + 0.02 * jax.random.normal(ks[11], (DEPTH, DA_V_DIM), f32)
    w_out = jax.random.normal(ks[12], (DEPTH, MIX_WIDTH, D_MODEL), f32) * (MIX_WIDTH ** -0.5) * DEEPNORM_BETA
    ln1_g = 1.0 + 0.02 * jax.random.normal(ks[13], (DEPTH, D_MODEL), f32)
    ln1_b = 0.02 * jax.random.normal(ks[14], (DEPTH, D_MODEL), f32)
    w_ffn_in = jax.random.normal(ks[15], (DEPTH, D_MODEL, 2 * D_FF), f32) * (D_MODEL ** -0.5) * DEEPNORM_BETA
    w_ffn_out = jax.random.normal(ks[16], (DEPTH, D_FF, D_MODEL), f32) * (D_FF ** -0.5) * DEEPNORM_BETA
    ln2_g = 1.0 + 0.02 * jax.random.normal(ks[17], (DEPTH, D_MODEL), f32)
    ln2_b = 0.02 * jax.random.normal(ks[18], (DEPTH, D_MODEL), f32)
    return {"x": x, "w_in": w_in, "conv_w": conv_w, "conv_b": conv_b,
            "lru_w_a": lru_w_a, "lru_b_a": lru_b_a, "lru_w_x": lru_w_x, "lru_b_x": lru_b_x,
            "lru_lambda": lru_lambda, "rec_norm_g": rec_norm_g, "da_lambda": da_lambda,
            "da_norm_g": da_norm_g, "w_out": w_out, "ln1_g": ln1_g, "ln1_b": ln1_b,
            "w_ffn_in": w_ffn_in, "w_ffn_out": w_ffn_out, "ln2_g": ln2_g, "ln2_b": ln2_b}


def reference(x, w_in, conv_w, conv_b, lru_w_a, lru_b_a, lru_w_x, lru_b_x, lru_lambda,
              rec_norm_g, da_lambda, da_norm_g, w_out, ln1_g, ln1_b, w_ffn_in, w_ffn_out,
              ln2_g, ln2_b):
    for l in range(DEPTH):
        lambda_init = 0.8 - 0.6 * math.exp(-0.3 * l)
        mix = hybrid_mixer(x, w_in[l], conv_w[l], conv_b[l], lru_w_a[l], lru_b_a[l],
                           lru_w_x[l], lru_b_x[l], lru_lambda[l], rec_norm_g[l],
                           da_lambda[l], da_norm_g[l], w_out[l], lambda_init)
        x = layer_norm(DEEPNORM_ALPHA * x + mix, ln1_g[l], ln1_b[l])
        ffn = swiglu_ffn(x, w_ffn_in[l], w_ffn_out[l])
        x = layer_norm(DEEPNORM_ALPHA * x + ffn, ln2_g[l], ln2_b[l])
    return x
```

```python
import functools
import math

import jax
import jax.numpy as jnp
from jax import lax
from jax.experimental import pallas as pl
from jax.experimental.pallas import tpu as pltpu

D_MODEL = 1024
ATTN_WIDTH = 512
REC_WIDTH = 512
DA_HEAD_DIM = 64
DA_V_DIM = 128
DA_N_HEADS = 4
REC_BLOCKS = 8
REC_BLOCK_DIM = 64
CONV_WIDTH = 4
LRU_C = 8.0
D_FF = 2816
QKV_WIDTH = 3 * ATTN_WIDTH
LN_EPS = 1e-5
RMS_EPS = 1e-5

LANES = 128
SUBLANES = 8
MXU_DIM = 256
VMEM_LIMIT_BYTES = 56 * 1024 * 1024

PROJ_ROWS = 512
ATTN_BLOCK = 256
REC_CHUNK = 256
FFN_ROWS = 256

NEG_BIG = -1e30
BF16 = jnp.bfloat16
F32 = jnp.float32


def _layer_norm(z, g, b):
    mu = jnp.mean(z, axis=-1, keepdims=True)
    zc = z - mu
    var = jnp.mean(zc * zc, axis=-1, keepdims=True)
    return zc * lax.rsqrt(var + LN_EPS) * g + b


def _in_proj_kernel(x_ref, w_ref, qkv_ref, rec_ref):
    xb = x_ref[...].astype(BF16)
    p = jnp.dot(xb, w_ref[...], preferred_element_type=F32)
    q = p[:, :ATTN_WIDTH] * (DA_HEAD_DIM ** -0.5)
    qkv_ref[:, :ATTN_WIDTH] = q.astype(BF16)
    qkv_ref[:, ATTN_WIDTH:] = p[:, ATTN_WIDTH:QKV_WIDTH].astype(BF16)
    rec_ref[...] = p[:, QKV_WIDTH:]


def _in_proj(x2d, w_in_bf16):
    rows = x2d.shape[0]
    in_width = w_in_bf16.shape[1]
    return pl.pallas_call(
        _in_proj_kernel,
        grid=(rows // PROJ_ROWS,),
        in_specs=[
            pl.BlockSpec((PROJ_ROWS, D_MODEL), lambda i: (i, 0)),
            pl.BlockSpec((D_MODEL, in_width), lambda i: (0, 0)),
        ],
        out_specs=[
            pl.BlockSpec((PROJ_ROWS, QKV_WIDTH), lambda i: (i, 0)),
            pl.BlockSpec((PROJ_ROWS, 2 * REC_WIDTH), lambda i: (i, 0)),
        ],
        out_shape=[
            jax.ShapeDtypeStruct((rows, QKV_WIDTH), BF16),
            jax.ShapeDtypeStruct((rows, 2 * REC_WIDTH), F32),
        ],
        compiler_params=pltpu.CompilerParams(
            dimension_semantics=("arbitrary",), vmem_limit_bytes=VMEM_LIMIT_BYTES),
        name="in_proj",
    )(x2d, w_in_bf16)


def _attn_kernel(lam_ref, g_ref, q_ref, k_ref, v_ref, o_ref,
                 ka_ref, kb_ref, m1_ref, l1_ref, a1_ref, m2_ref, l2_ref, a2_ref,
                 *, seq, lambda_init):
    blk = ATTN_BLOCK
    lane = lax.broadcasted_iota(jnp.int32, (1, DA_V_DIM), 1)
    lo = (lane < DA_HEAD_DIM).astype(BF16)
    k_all = k_ref[...]
    ka_ref[...] = k_all * lo
    kb_ref[...] = k_all * (1 - lo)

    lp = lam_ref[...]
    t1 = jnp.sum(lp[0:1, :] * lp[1:2, :], axis=-1, keepdims=True)
    t2 = jnp.sum(lp[2:3, :] * lp[3:4, :], axis=-1, keepdims=True)
    lam = jnp.exp(t1) - jnp.exp(t2) + lambda_init

    row = lax.broadcasted_iota(jnp.int32, (blk, blk), 0)
    col = lax.broadcasted_iota(jnp.int32, (blk, blk), 1)
    causal = row >= col

    def update(s, v, m_ref, l_ref, a_ref):
        m_prev = m_ref[...]
        m_new = jnp.maximum(m_prev, jnp.max(s, axis=-1, keepdims=True))
        alpha = jnp.exp(m_prev - m_new)
        p = jnp.exp(s - m_new)
        l_ref[...] = alpha * l_ref[...] + jnp.sum(p, axis=-1, keepdims=True)
        a_ref[...] = alpha * a_ref[...] + jnp.dot(
            p.astype(BF16), v, preferred_element_type=F32)
        m_ref[...] = m_new

    def kv_step(q, j, diag):
        start = pl.multiple_of(j * blk, blk)
        ka = ka_ref[pl.ds(start, blk), :]
        kb = kb_ref[pl.ds(start, blk), :]
        v = v_ref[pl.ds(start, blk), :]
        dims = (((1,), (1,)), ((), ()))
        s1 = lax.dot_general(q, ka, dims, preferred_element_type=F32)
        s2 = lax.dot_general(q, kb, dims, preferred_element_type=F32)
        if diag:
            s1 = jnp.where(causal, s1, NEG_BIG)
            s2 = jnp.where(causal, s2, NEG_BIG)
        update(s1, v, m1_ref, l1_ref, a1_ref)
        update(s2, v, m2_ref, l2_ref, a2_ref)

    def q_block(i, carry):
        qstart = pl.multiple_of(i * blk, blk)
        q = q_ref[pl.ds(qstart, blk), :]
        for m_ref, l_ref, a_ref in ((m1_ref, l1_ref, a1_ref), (m2_ref, l2_ref, a2_ref)):
            m_ref[...] = jnp.full(m_ref.shape, NEG_BIG, F32)
            l_ref[...] = jnp.zeros(l_ref.shape, F32)
            a_ref[...] = jnp.zeros(a_ref.shape, F32)

        def kv_body(j, c):
            kv_step(q, j, False)
            return c

        lax.fori_loop(0, i, kv_body, 0)
        kv_step(q, i, True)

        o = a1_ref[...] / l1_ref[...] - lam * (a2_ref[...] / l2_ref[...])
        ms = jnp.mean(o * o, axis=-1, keepdims=True)
        o = o * lax.rsqrt(ms + RMS_EPS) * g_ref[...] * (1.0 - lambda_init)
        o_ref[pl.ds(qstart, blk), :] = o.astype(o_ref.dtype)
        return carry

    lax.fori_loop(0, seq // blk, q_block, 0)


def _diff_attention(qkv, da_lam, da_g, lambda_init):
    batch, seq, _ = qkv.shape
    blk = ATTN_BLOCK
    kernel = functools.partial(_attn_kernel, seq=seq, lambda_init=lambda_init)
    head_blocks = ATTN_WIDTH // DA_V_DIM
    return pl.pallas_call(
        kernel,
        grid=(batch, DA_N_HEADS),
        in_specs=[
            pl.BlockSpec((4, DA_HEAD_DIM), lambda b, h: (0, 0)),
            pl.BlockSpec((1, DA_V_DIM), lambda b, h: (0, 0)),
            pl.BlockSpec((None, seq, DA_V_DIM), lambda b, h: (b, 0, h)),
            pl.BlockSpec((None, seq, DA_V_DIM), lambda b, h: (b, 0, head_blocks + h)),
            pl.BlockSpec((None, seq, DA_V_DIM), lambda b, h: (b, 0, 2 * head_blocks + h)),
        ],
        out_specs=pl.BlockSpec((None, seq, DA_V_DIM), lambda b, h: (b, 0, h)),
        out_shape=jax.ShapeDtypeStruct((batch, seq, ATTN_WIDTH), BF16),
        scratch_shapes=[
            pltpu.VMEM((seq, DA_V_DIM), BF16),
            pltpu.VMEM((seq, DA_V_DIM), BF16),
            pltpu.VMEM((blk, 1), F32), pltpu.VMEM((blk, 1), F32), pltpu.VMEM((blk, DA_V_DIM), F32),
            pltpu.VMEM((blk, 1), F32), pltpu.VMEM((blk, 1), F32), pltpu.VMEM((blk, DA_V_DIM), F32),
        ],
        compiler_params=pltpu.CompilerParams(
            dimension_semantics=("arbitrary", "arbitrary"), vmem_limit_bytes=VMEM_LIMIT_BYTES),
        name="diff_attention",
    )(da_lam, da_g.reshape(1, DA_V_DIM), qkv, qkv, qkv)


def _gelu_tanh(x):
    c = math.sqrt(2.0 / math.pi)
    return x * (0.5 * (1.0 + jnp.tanh(c * (x + 0.044715 * (x * x * x)))))


def _rec_kernel(xr_ref, gate_ref, cw_ref, cb_ref, wa_ref, wx_ref, ba_ref, bx_ref,
                lam_ref, g_ref, o_ref, xprev_ref, hprev_ref):
    chunk = REC_CHUNK
    c = pl.program_id(1)

    @pl.when(c == 0)
    def _():
        xprev_ref[...] = jnp.zeros(xprev_ref.shape, F32)
        hprev_ref[...] = jnp.zeros(hprev_ref.shape, F32)

    x = xr_ref[...]
    ext = jnp.concatenate([xprev_ref[...], x], axis=0)
    xprev_ref[...] = x[chunk - SUBLANES:, :]
    cw = cw_ref[...]
    conv = cb_ref[...] + x * cw[CONV_WIDTH - 1:CONV_WIDTH, :]
    for tap in range(CONV_WIDTH - 1):
        off = SUBLANES - (CONV_WIDTH - 1) + tap
        conv = conv + ext[off:off + chunk, :] * cw[tap:tap + 1, :]

    xb = conv.astype(BF16)
    half = MXU_DIM

    def block_diag_dot(w_ref):
        lo = jnp.dot(xb[:, :half], w_ref[0], preferred_element_type=F32)
        hi = jnp.dot(xb[:, half:], w_ref[1], preferred_element_type=F32)
        return jnp.concatenate([lo, hi], axis=1)

    r = jax.nn.sigmoid(block_diag_dot(wa_ref) + ba_ref[...])
    gi = jax.nn.sigmoid(block_diag_dot(wx_ref) + bx_ref[...])
    z = -lam_ref[...]
    softplus = jnp.maximum(z, 0.0) + jnp.log1p(jnp.exp(-jnp.abs(z)))
    log_a = -LRU_C * r * softplus
    a = jnp.exp(log_a)
    mult = jnp.sqrt(-jnp.tanh(log_a) * (1.0 + a * a))
    row = lax.broadcasted_iota(jnp.int32, (chunk, 1), 0)
    mult = jnp.where(row + c * chunk == 0, 1.0, mult)
    bt = mult * (gi * conv)

    shift = 1
    while shift < chunk:
        valid = row >= shift
        a_sh = pltpu.roll(a, shift, axis=0)
        b_sh = pltpu.roll(bt, shift, axis=0)
        bt = jnp.where(valid, a * b_sh + bt, bt)
        a = jnp.where(valid, a * a_sh, a)
        shift *= 2
    h = a * hprev_ref[SUBLANES - 1:SUBLANES, :] + bt
    hprev_ref[...] = h[chunk - SUBLANES:, :]

    y = h * _gelu_tanh(gate_ref[...])
    ms = jnp.mean(y * y, axis=-1, keepdims=True)
    o_ref[...] = (y * lax.rsqrt(ms + RMS_EPS) * g_ref[...]).astype(o_ref.dtype)


def _block_diag_halves(w):
    per_tile = MXU_DIM // REC_BLOCK_DIM
    tiles = []
    for t in range(REC_BLOCKS // per_tile):
        tiles.append(jax.scipy.linalg.block_diag(*[w[t * per_tile + n] for n in range(per_tile)]))
    return jnp.stack(tiles).astype(BF16)


def _rec_branch(rec, conv_w, conv_b, w_a, b_a, w_x, b_x, lru_lam, rec_g):
    batch, seq, _ = rec.shape
    chunk = REC_CHUNK
    row_vec = lambda p: p.reshape(1, REC_WIDTH)
    vec_spec = pl.BlockSpec((1, REC_WIDTH), lambda b, c: (0, 0))
    w_spec = pl.BlockSpec((2, MXU_DIM, MXU_DIM), lambda b, c: (0, 0, 0))
    return pl.pallas_call(
        _rec_kernel,
        grid=(batch, seq // chunk),
        in_specs=[
            pl.BlockSpec((None, chunk, REC_WIDTH), lambda b, c: (b, c, 0)),
            pl.BlockSpec((None, chunk, REC_WIDTH), lambda b, c: (b, c, 1)),
            pl.BlockSpec((CONV_WIDTH, REC_WIDTH), lambda b, c: (0, 0)),
            vec_spec, w_spec, w_spec, vec_spec, vec_spec, vec_spec, vec_spec,
        ],
        out_specs=pl.BlockSpec((None, chunk, REC_WIDTH), lambda b, c: (b, c, 0)),
        out_shape=jax.ShapeDtypeStruct((batch, seq, REC_WIDTH), BF16),
        scratch_shapes=[
            pltpu.VMEM((SUBLANES, REC_WIDTH), F32),
            pltpu.VMEM((SUBLANES, REC_WIDTH), F32),
        ],
        compiler_params=pltpu.CompilerParams(
            dimension_semantics=("arbitrary", "arbitrary"), vmem_limit_bytes=VMEM_LIMIT_BYTES),
        name="rg_lru",
    )(rec, rec, conv_w, row_vec(conv_b), _block_diag_halves(w_a), _block_diag_halves(w_x),
      row_vec(b_a), row_vec(b_x), row_vec(lru_lam), row_vec(rec_g))


def _out_ffn_kernel(x_ref, attn_ref, rec_ref, wo_a_ref, wo_r_ref, g1_ref, b1_ref,
                    w1_ref, w2_ref, g2_ref, b2_ref, o_ref, *, alpha):
    mix = jnp.dot(attn_ref[...], wo_a_ref[...], preferred_element_type=F32)
    mix = mix + jnp.dot(rec_ref[...], wo_r_ref[...], preferred_element_type=F32)
    x1 = _layer_norm(alpha * x_ref[...] + mix, g1_ref[...], b1_ref[...])
    hu = jnp.dot(x1.astype(BF16), w1_ref[...], preferred_element_type=F32)
    g = hu[:, :D_FF]
    u = hu[:, D_FF:]
    act = (g * jax.nn.sigmoid(g) * u).astype(BF16)
    ffn = jnp.dot(act, w2_ref[...], preferred_element_type=F32)
    o_ref[...] = _layer_norm(alpha * x1 + ffn, g2_ref[...], b2_ref[...])


def _out_ffn(x2d, attn2d, rec2d, w_out, ln1_g, ln1_b, w_ffn_in, w_ffn_out, ln2_g, ln2_b, alpha):
    rows = x2d.shape[0]
    tm = FFN_ROWS
    row_vec = lambda p: p.reshape(1, D_MODEL)
    const = lambda shape: pl.BlockSpec(shape, lambda i: (0,) * len(shape),
                                       pipeline_mode=pl.Buffered(1))
    w_out_b = w_out.astype(BF16)
    return pl.pallas_call(
        functools.partial(_out_ffn_kernel, alpha=alpha),
        grid=(rows // tm,),
        in_specs=[
            pl.BlockSpec((tm, D_MODEL), lambda i: (i, 0)),
            pl.BlockSpec((tm, ATTN_WIDTH), lambda i: (i, 0)),
            pl.BlockSpec((tm, REC_WIDTH), lambda i: (i, 0)),
            const((ATTN_WIDTH, D_MODEL)), const((REC_WIDTH, D_MODEL)),
            const((1, D_MODEL)), const((1, D_MODEL)),
            const((D_MODEL, 2 * D_FF)), const((D_FF, D_MODEL)),
            const((1, D_MODEL)), const((1, D_MODEL)),
        ],
        out_specs=pl.BlockSpec((tm, D_MODEL), lambda i: (i, 0)),
        out_shape=jax.ShapeDtypeStruct((rows, D_MODEL), F32),
        compiler_params=pltpu.CompilerParams(
            dimension_semantics=("arbitrary",), vmem_limit_bytes=VMEM_LIMIT_BYTES),
        name="out_ffn",
    )(x2d, attn2d, rec2d, w_out_b[:ATTN_WIDTH], w_out_b[ATTN_WIDTH:], row_vec(ln1_g), row_vec(ln1_b),
      w_ffn_in.astype(BF16), w_ffn_out.astype(BF16), row_vec(ln2_g), row_vec(ln2_b))


def kernel(x, w_in, conv_w, conv_b, lru_w_a, lru_b_a, lru_w_x, lru_b_x, lru_lambda,
           rec_norm_g, da_lambda, da_norm_g, w_out, ln1_g, ln1_b, w_ffn_in, w_ffn_out,
           ln2_g, ln2_b):
    batch, seq, _ = x.shape
    depth = w_in.shape[0]
    alpha = (2.0 * depth) ** 0.25
    x2d = x.reshape(batch * seq, D_MODEL)
    for l in range(depth):
        lambda_init = 0.8 - 0.6 * math.exp(-0.3 * l)
        qkv, rec = _in_proj(x2d, w_in[l].astype(BF16))
        attn = _diff_attention(qkv.reshape(batch, seq, QKV_WIDTH), da_lambda[l], da_norm_g[l],
                               lambda_init)
        rec_out = _rec_branch(rec.reshape(batch, seq, 2 * REC_WIDTH), conv_w[l], conv_b[l],
                              lru_w_a[l], lru_b_a[l], lru_w_x[l], lru_b_x[l], lru_lambda[l],
                              rec_norm_g[l])
        x2d = _out_ffn(x2d, attn.reshape(batch * seq, ATTN_WIDTH),
                       rec_out.reshape(batch * seq, REC_WIDTH), w_out[l], ln1_g[l], ln1_b[l],
                       w_ffn_in[l], w_ffn_out[l], ln2_g[l], ln2_b[l], alpha)
    return x2d.reshape(batch, seq, D_MODEL)
```

```python
import functools
import math

import jax
import jax.numpy as jnp
from jax import lax
from jax.experimental import pallas as pl
from jax.experimental.pallas import tpu as pltpu

D_MODEL = 1024
ATTN_WIDTH = 512
REC_WIDTH = 512
DA_HEAD_DIM = 64
DA_V_DIM = 128
DA_N_HEADS = 4
REC_BLOCKS = 8
REC_BLOCK_DIM = 64
CONV_WIDTH = 4
LRU_C = 8.0
D_FF = 2816
QKV_WIDTH = 3 * ATTN_WIDTH
LN_EPS = 1e-5
RMS_EPS = 1e-5

LANES = 128
SUBLANES = 8
MXU_DIM = 256
VMEM_LIMIT_BYTES = 56 * 1024 * 1024

PROJ_ROWS = 512
ATTN_BLOCK = 256
REC_CHUNK = 256
FFN_ROWS = 256

NEG_BIG = -1e30
BF16 = jnp.bfloat16
F32 = jnp.float32


def _layer_norm(z, g, b):
    mu = jnp.mean(z, axis=-1, keepdims=True)
    zc = z - mu
    var = jnp.mean(zc * zc, axis=-1, keepdims=True)
    return zc * lax.rsqrt(var + LN_EPS) * g + b


def _in_proj_kernel(x_ref, w_ref, qkv_ref, rec_ref):
    xb = x_ref[...].astype(BF16)
    p = jnp.dot(xb, w_ref[...], preferred_element_type=F32)
    q = p[:, :ATTN_WIDTH] * (DA_HEAD_DIM ** -0.5)
    qkv_ref[:, :ATTN_WIDTH] = q.astype(BF16)
    qkv_ref[:, ATTN_WIDTH:] = p[:, ATTN_WIDTH:QKV_WIDTH].astype(BF16)
    rec_ref[...] = p[:, QKV_WIDTH:]


def _in_proj(x2d, w_in_bf16):
    rows = x2d.shape[0]
    in_width = w_in_bf16.shape[1]
    return pl.pallas_call(
        _in_proj_kernel,
        grid=(rows // PROJ_ROWS,),
        in_specs=[
            pl.BlockSpec((PROJ_ROWS, D_MODEL), lambda i: (i, 0)),
            pl.BlockSpec((D_MODEL, in_width), lambda i: (0, 0)),
        ],
        out_specs=[
            pl.BlockSpec((PROJ_ROWS, QKV_WIDTH), lambda i: (i, 0)),
            pl.BlockSpec((PROJ_ROWS, 2 * REC_WIDTH), lambda i: (i, 0)),
        ],
        out_shape=[
            jax.ShapeDtypeStruct((rows, QKV_WIDTH), BF16),
            jax.ShapeDtypeStruct((rows, 2 * REC_WIDTH), F32),
        ],
        compiler_params=pltpu.CompilerParams(
            dimension_semantics=("arbitrary",), vmem_limit_bytes=VMEM_LIMIT_BYTES),
        name="in_proj",
    )(x2d, w_in_bf16)


def _attn_kernel(lam_ref, g_ref, q_ref, k_ref, v_ref, o_ref, ka_ref, kb_ref, vext_ref,
                 *, seq, lambda_init):
    blk = ATTN_BLOCK
    lane = lax.broadcasted_iota(jnp.int32, (1, DA_V_DIM), 1)
    lo = (lane < DA_HEAD_DIM).astype(BF16)
    k_all = k_ref[...]
    ka_ref[...] = k_all * lo
    kb_ref[...] = k_all * (1 - lo)
    vext_ref[:, :DA_V_DIM] = v_ref[...]
    vext_ref[:, DA_V_DIM:] = jnp.ones((seq, DA_V_DIM), BF16)

    lp = lam_ref[...]
    t1 = jnp.sum(lp[0:1, :] * lp[1:2, :], axis=-1, keepdims=True)
    t2 = jnp.sum(lp[2:3, :] * lp[3:4, :], axis=-1, keepdims=True)
    lam = jnp.exp(t1) - jnp.exp(t2) + lambda_init

    row = lax.broadcasted_iota(jnp.int32, (blk, blk), 0)
    col = lax.broadcasted_iota(jnp.int32, (blk, blk), 1)
    causal = row >= col
    dims = (((1,), (1,)), ((), ()))

    def one_map(q, kx_ref, kv_len):
        s = lax.dot_general(q, kx_ref[:kv_len, :], dims, preferred_element_type=F32)
        s_diag = jnp.where(causal, s[:, kv_len - blk:], NEG_BIG)
        if kv_len > blk:
            s = jnp.concatenate([s[:, :kv_len - blk], s_diag], axis=1)
        else:
            s = s_diag
        m = jnp.max(s, axis=-1, keepdims=True)
        p = jnp.exp(s - m).astype(BF16)
        r = jnp.dot(p, vext_ref[:kv_len, :], preferred_element_type=F32)
        return r[:, :DA_V_DIM] / r[:, DA_V_DIM:]

    for i in range(seq // blk):
        kv_len = (i + 1) * blk
        q = q_ref[i * blk:kv_len, :]
        o = one_map(q, ka_ref, kv_len) - lam * one_map(q, kb_ref, kv_len)
        ms = jnp.mean(o * o, axis=-1, keepdims=True)
        o = o * lax.rsqrt(ms + RMS_EPS) * g_ref[...] * (1.0 - lambda_init)
        o_ref[i * blk:kv_len, :] = o.astype(o_ref.dtype)


def _diff_attention(qkv, da_lam, da_g, lambda_init):
    batch, seq, _ = qkv.shape
    kernel = functools.partial(_attn_kernel, seq=seq, lambda_init=lambda_init)
    head_blocks = ATTN_WIDTH // DA_V_DIM
    return pl.pallas_call(
        kernel,
        grid=(batch, DA_N_HEADS),
        in_specs=[
            pl.BlockSpec((4, DA_HEAD_DIM), lambda b, h: (0, 0)),
            pl.BlockSpec((1, DA_V_DIM), lambda b, h: (0, 0)),
            pl.BlockSpec((None, seq, DA_V_DIM), lambda b, h: (b, 0, h)),
            pl.BlockSpec((None, seq, DA_V_DIM), lambda b, h: (b, 0, head_blocks + h)),
            pl.BlockSpec((None, seq, DA_V_DIM), lambda b, h: (b, 0, 2 * head_blocks + h)),
        ],
        out_specs=pl.BlockSpec((None, seq, DA_V_DIM), lambda b, h: (b, 0, h)),
        out_shape=jax.ShapeDtypeStruct((batch, seq, ATTN_WIDTH), BF16),
        scratch_shapes=[
            pltpu.VMEM((seq, DA_V_DIM), BF16),
            pltpu.VMEM((seq, DA_V_DIM), BF16),
            pltpu.VMEM((seq, 2 * DA_V_DIM), BF16),
        ],
        compiler_params=pltpu.CompilerParams(
            dimension_semantics=("arbitrary", "arbitrary"), vmem_limit_bytes=VMEM_LIMIT_BYTES),
        name="diff_attention",
    )(da_lam, da_g.reshape(1, DA_V_DIM), qkv, qkv, qkv)


def _gelu_tanh(x):
    c = math.sqrt(2.0 / math.pi)
    return x * (0.5 * (1.0 + jnp.tanh(c * (x + 0.044715 * (x * x * x)))))


def _rec_kernel(xr_ref, gate_ref, cw_ref, cb_ref, wa_ref, wx_ref, ba_ref, bx_ref,
                lam_ref, g_ref, o_ref, xprev_ref, hprev_ref):
    chunk = REC_CHUNK
    c = pl.program_id(1)

    @pl.when(c == 0)
    def _():
        xprev_ref[...] = jnp.zeros(xprev_ref.shape, F32)
        hprev_ref[...] = jnp.zeros(hprev_ref.shape, F32)

    x = xr_ref[...]
    ext = jnp.concatenate([xprev_ref[...], x], axis=0)
    xprev_ref[...] = x[chunk - SUBLANES:, :]
    cw = cw_ref[...]
    conv = cb_ref[...] + x * cw[CONV_WIDTH - 1:CONV_WIDTH, :]
    for tap in range(CONV_WIDTH - 1):
        off = SUBLANES - (CONV_WIDTH - 1) + tap
        conv = conv + ext[off:off + chunk, :] * cw[tap:tap + 1, :]

    xb = conv.astype(BF16)
    half = MXU_DIM

    def block_diag_dot(w_ref):
        lo = jnp.dot(xb[:, :half], w_ref[0], preferred_element_type=F32)
        hi = jnp.dot(xb[:, half:], w_ref[1], preferred_element_type=F32)
        return jnp.concatenate([lo, hi], axis=1)

    r = jax.nn.sigmoid(block_diag_dot(wa_ref) + ba_ref[...])
    gi = jax.nn.sigmoid(block_diag_dot(wx_ref) + bx_ref[...])
    z = -lam_ref[...]
    softplus = jnp.maximum(z, 0.0) + jnp.log1p(jnp.exp(-jnp.abs(z)))
    log_a = -LRU_C * r * softplus
    a = jnp.exp(log_a)
    mult = jnp.sqrt(-jnp.tanh(log_a) * (1.0 + a * a))
    row = lax.broadcasted_iota(jnp.int32, (chunk, 1), 0)
    mult = jnp.where(row + c * chunk == 0, 1.0, mult)
    bt = mult * (gi * conv)

    shift = 1
    while shift < chunk:
        valid = row >= shift
        a_sh = pltpu.roll(a, shift, axis=0)
        b_sh = pltpu.roll(bt, shift, axis=0)
        bt = jnp.where(valid, a * b_sh + bt, bt)
        a = jnp.where(valid, a * a_sh, a)
        shift *= 2
    h = a * hprev_ref[SUBLANES - 1:SUBLANES, :] + bt
    hprev_ref[...] = h[chunk - SUBLANES:, :]

    y = h * _gelu_tanh(gate_ref[...])
    ms = jnp.mean(y * y, axis=-1, keepdims=True)
    o_ref[...] = (y * lax.rsqrt(ms + RMS_EPS) * g_ref[...]).astype(o_ref.dtype)


def _block_diag_halves(w):
    per_tile = MXU_DIM // REC_BLOCK_DIM
    tiles = []
    for t in range(REC_BLOCKS // per_tile):
        tiles.append(jax.scipy.linalg.block_diag(*[w[t * per_tile + n] for n in range(per_tile)]))
    return jnp.stack(tiles).astype(BF16)


def _rec_branch(rec, conv_w, conv_b, w_a, b_a, w_x, b_x, lru_lam, rec_g):
    batch, seq, _ = rec.shape
    chunk = REC_CHUNK
    row_vec = lambda p: p.reshape(1, REC_WIDTH)
    vec_spec = pl.BlockSpec((1, REC_WIDTH), lambda b, c: (0, 0))
    w_spec = pl.BlockSpec((2, MXU_DIM, MXU_DIM), lambda b, c: (0, 0, 0))
    return pl.pallas_call(
        _rec_kernel,
        grid=(batch, seq // chunk),
        in_specs=[
            pl.BlockSpec((None, chunk, REC_WIDTH), lambda b, c: (b, c, 0)),
            pl.BlockSpec((None, chunk, REC_WIDTH), lambda b, c: (b, c, 1)),
            pl.BlockSpec((CONV_WIDTH, REC_WIDTH), lambda b, c: (0, 0)),
            vec_spec, w_spec, w_spec, vec_spec, vec_spec, vec_spec, vec_spec,
        ],
        out_specs=pl.BlockSpec((None, chunk, REC_WIDTH), lambda b, c: (b, c, 0)),
        out_shape=jax.ShapeDtypeStruct((batch, seq, REC_WIDTH), BF16),
        scratch_shapes=[
            pltpu.VMEM((SUBLANES, REC_WIDTH), F32),
            pltpu.VMEM((SUBLANES, REC_WIDTH), F32),
        ],
        compiler_params=pltpu.CompilerParams(
            dimension_semantics=("arbitrary", "arbitrary"), vmem_limit_bytes=VMEM_LIMIT_BYTES),
        name="rg_lru",
    )(rec, rec, conv_w, row_vec(conv_b), _block_diag_halves(w_a), _block_diag_halves(w_x),
      row_vec(b_a), row_vec(b_x), row_vec(lru_lam), row_vec(rec_g))


def _out_ffn_kernel(x_ref, attn_ref, rec_ref, wo_a_ref, wo_r_ref, g1_ref, b1_ref,
                    w1_ref, w2_ref, g2_ref, b2_ref, o_ref, *, alpha):
    mix = jnp.dot(attn_ref[...], wo_a_ref[...], preferred_element_type=F32)
    mix = mix + jnp.dot(rec_ref[...], wo_r_ref[...], preferred_element_type=F32)
    x1 = _layer_norm(alpha * x_ref[...] + mix, g1_ref[...], b1_ref[...])
    hu = jnp.dot(x1.astype(BF16), w1_ref[...], preferred_element_type=F32)
    g = hu[:, :D_FF]
    u = hu[:, D_FF:]
    act = (g * jax.nn.sigmoid(g) * u).astype(BF16)
    ffn = jnp.dot(act, w2_ref[...], preferred_element_type=F32)
    o_ref[...] = _layer_norm(alpha * x1 + ffn, g2_ref[...], b2_ref[...])


def _out_ffn(x2d, attn2d, rec2d, w_out, ln1_g, ln1_b, w_ffn_in, w_ffn_out, ln2_g, ln2_b, alpha):
    rows = x2d.shape[0]
    tm = FFN_ROWS
    row_vec = lambda p: p.reshape(1, D_MODEL)
    const = lambda shape: pl.BlockSpec(shape, lambda i: (0,) * len(shape),
                                       pipeline_mode=pl.Buffered(1))
    w_out_b = w_out.astype(BF16)
    return pl.pallas_call(
        functools.partial(_out_ffn_kernel, alpha=alpha),
        grid=(rows // tm,),
        in_specs=[
            pl.BlockSpec((tm, D_MODEL), lambda i: (i, 0)),
            pl.BlockSpec((tm, ATTN_WIDTH), lambda i: (i, 0)),
            pl.BlockSpec((tm, REC_WIDTH), lambda i: (i, 0)),
            const((ATTN_WIDTH, D_MODEL)), const((REC_WIDTH, D_MODEL)),
            const((1, D_MODEL)), const((1, D_MODEL)),
            const((D_MODEL, 2 * D_FF)), const((D_FF, D_MODEL)),
            const((1, D_MODEL)), const((1, D_MODEL)),
        ],
        out_specs=pl.BlockSpec((tm, D_MODEL), lambda i: (i, 0)),
        out_shape=jax.ShapeDtypeStruct((rows, D_MODEL), F32),
        compiler_params=pltpu.CompilerParams(
            dimension_semantics=("arbitrary",), vmem_limit_bytes=VMEM_LIMIT_BYTES),
        name="out_ffn",
    )(x2d, attn2d, rec2d, w_out_b[:ATTN_WIDTH], w_out_b[ATTN_WIDTH:], row_vec(ln1_g), row_vec(ln1_b),
      w_ffn_in.astype(BF16), w_ffn_out.astype(BF16), row_vec(ln2_g), row_vec(ln2_b))


def kernel(x, w_in, conv_w, conv_b, lru_w_a, lru_b_a, lru_w_x, lru_b_x, lru_lambda,
           rec_norm_g, da_lambda, da_norm_g, w_out, ln1_g, ln1_b, w_ffn_in, w_ffn_out,
           ln2_g, ln2_b):
    batch, seq, _ = x.shape
    depth = w_in.shape[0]
    alpha = (2.0 * depth) ** 0.25
    x2d = x.reshape(batch * seq, D_MODEL)
    for l in range(depth):
        lambda_init = 0.8 - 0.6 * math.exp(-0.3 * l)
        qkv, rec = _in_proj(x2d, w_in[l].astype(BF16))
        attn = _diff_attention(qkv.reshape(batch, seq, QKV_WIDTH), da_lambda[l], da_norm_g[l],
                               lambda_init)
        rec_out = _rec_branch(rec.reshape(batch, seq, 2 * REC_WIDTH), conv_w[l], conv_b[l],
                              lru_w_a[l], lru_b_a[l], lru_w_x[l], lru_b_x[l], lru_lambda[l],
                              rec_norm_g[l])
        x2d = _out_ffn(x2d, attn.reshape(batch * seq, ATTN_WIDTH),
                       rec_out.reshape(batch * seq, REC_WIDTH), w_out[l], ln1_g[l], ln1_b[l],
                       w_ffn_in[l], w_ffn_out[l], ln2_g[l], ln2_b[l], alpha)
    return x2d.reshape(batch, seq, D_MODEL)
```

```python
import functools
import math

import jax
import jax.numpy as jnp
from jax import lax
from jax.experimental import pallas as pl
from jax.experimental.pallas import tpu as pltpu

D_MODEL = 1024
ATTN_WIDTH = 512
REC_WIDTH = 512
DA_HEAD_DIM = 64
DA_V_DIM = 128
DA_N_HEADS = 4
REC_BLOCKS = 8
REC_BLOCK_DIM = 64
CONV_WIDTH = 4
LRU_C = 8.0
D_FF = 2816
QKV_WIDTH = 3 * ATTN_WIDTH
LN_EPS = 1e-5
RMS_EPS = 1e-5

LANES = 128
SUBLANES = 8
MXU_DIM = 256
VMEM_LIMIT_BYTES = 56 * 1024 * 1024

PROJ_ROWS = 512
ATTN_BLOCK = 256
ONES_ROWS = 16
REC_CHUNK = 256
FFN_ROWS = 512
FFN_SUB_ROWS = 256

NEG_BIG = -1e30
LOG2_E = 1.0 / math.log(2.0)
BF16 = jnp.bfloat16
F32 = jnp.float32


def _layer_norm(z, g, b):
    mu = jnp.mean(z, axis=-1, keepdims=True)
    zc = z - mu
    var = jnp.mean(zc * zc, axis=-1, keepdims=True)
    return zc * lax.rsqrt(var + LN_EPS) * g + b


def _in_proj_kernel(x_ref, w_ref, qkv_ref, rec_ref):
    xb = x_ref[...].astype(BF16)
    p = jnp.dot(xb, w_ref[...], preferred_element_type=F32)
    q = p[:, :ATTN_WIDTH] * (DA_HEAD_DIM ** -0.5 * LOG2_E)
    qkv_ref[:, :ATTN_WIDTH] = q.astype(BF16)
    qkv_ref[:, ATTN_WIDTH:] = p[:, ATTN_WIDTH:QKV_WIDTH].astype(BF16)
    rec_ref[...] = p[:, QKV_WIDTH:]


def _in_proj(x2d, w_in_bf16):
    rows = x2d.shape[0]
    in_width = w_in_bf16.shape[1]
    return pl.pallas_call(
        _in_proj_kernel,
        grid=(rows // PROJ_ROWS,),
        in_specs=[
            pl.BlockSpec((PROJ_ROWS, D_MODEL), lambda i: (i, 0)),
            pl.BlockSpec((D_MODEL, in_width), lambda i: (0, 0)),
        ],
        out_specs=[
            pl.BlockSpec((PROJ_ROWS, QKV_WIDTH), lambda i: (i, 0)),
            pl.BlockSpec((PROJ_ROWS, 2 * REC_WIDTH), lambda i: (i, 0)),
        ],
        out_shape=[
            jax.ShapeDtypeStruct((rows, QKV_WIDTH), BF16),
            jax.ShapeDtypeStruct((rows, 2 * REC_WIDTH), F32),
        ],
        compiler_params=pltpu.CompilerParams(
            dimension_semantics=("arbitrary",), vmem_limit_bytes=VMEM_LIMIT_BYTES),
        name="in_proj",
    )(x2d, w_in_bf16)


def _attn_kernel(lam_ref, g_ref, q_ref, k_ref, v_ref, o_ref, ka_ref, kb_ref, vext_ref,
                 *, seq, lambda_init):
    blk = ATTN_BLOCK
    lane = lax.broadcasted_iota(jnp.int32, (1, DA_V_DIM), 1)
    lo = (lane < DA_HEAD_DIM).astype(BF16)
    k_all = k_ref[...]
    ka_ref[...] = k_all * lo
    kb_ref[...] = k_all * (1 - lo)
    vext_ref[:DA_V_DIM, :] = v_ref[...].astype(F32).T.astype(BF16)
    vext_ref[DA_V_DIM:, :] = jnp.ones((ONES_ROWS, seq), BF16)

    lp = lam_ref[...]
    t1 = jnp.sum(lp[0:1, :] * lp[1:2, :], axis=-1, keepdims=True)
    t2 = jnp.sum(lp[2:3, :] * lp[3:4, :], axis=-1, keepdims=True)
    lam = jnp.exp(t1) - jnp.exp(t2) + lambda_init

    kv_idx = lax.broadcasted_iota(jnp.int32, (blk, blk), 0)
    q_idx = lax.broadcasted_iota(jnp.int32, (blk, blk), 1)
    causal = kv_idx <= q_idx
    dims = (((1,), (1,)), ((), ()))

    def scores(i, kx_ref):
        kv_len = (i + 1) * blk
        q = q_ref[i * blk:kv_len, :]
        s = lax.dot_general(kx_ref[:kv_len, :], q, dims, preferred_element_type=F32)
        s_diag = jnp.where(causal, s[kv_len - blk:, :], NEG_BIG)
        if kv_len > blk:
            s = jnp.concatenate([s[:kv_len - blk, :], s_diag], axis=0)
        else:
            s = s_diag
        return s

    def softmax_numerator(s):
        m = jnp.max(s, axis=0, keepdims=True)
        return jnp.exp2(s - m).astype(BF16)

    def weighted_values(i, p):
        kv_len = (i + 1) * blk
        r = jnp.dot(vext_ref[:, :kv_len], p, preferred_element_type=F32)
        return r[:DA_V_DIM, :] / r[DA_V_DIM:DA_V_DIM + 1, :]

    n_blocks = seq // blk
    chains = [(i, kx_ref) for i in reversed(range(n_blocks)) for kx_ref in (ka_ref, kb_ref)]
    n_chains = len(chains)
    s_vals = {0: scores(*chains[0]), 1: scores(*chains[1])}
    p_vals = {0: softmax_numerator(s_vals.pop(0))}
    outs = []
    for c, (i, _) in enumerate(chains):
        if c + 2 < n_chains:
            s_vals[c + 2] = scores(*chains[c + 2])
        if c + 1 < n_chains:
            p_vals[c + 1] = softmax_numerator(s_vals.pop(c + 1))
        outs.append(weighted_values(i, p_vals.pop(c)))
        if len(outs) == 2:
            o_t = outs[0] - lam * outs[1]
            outs = []
            ms = jnp.mean(o_t * o_t, axis=0, keepdims=True)
            o = (o_t * lax.rsqrt(ms + RMS_EPS)).T
            o = o * g_ref[...] * (1.0 - lambda_init)
            o_ref[i * blk:(i + 1) * blk, :] = o.astype(o_ref.dtype)


def _diff_attention(qkv, da_lam, da_g, lambda_init):
    batch, seq, _ = qkv.shape
    kernel = functools.partial(_attn_kernel, seq=seq, lambda_init=lambda_init)
    head_blocks = ATTN_WIDTH // DA_V_DIM
    return pl.pallas_call(
        kernel,
        grid=(batch, DA_N_HEADS),
        in_specs=[
            pl.BlockSpec((4, DA_HEAD_DIM), lambda b, h: (0, 0)),
            pl.BlockSpec((1, DA_V_DIM), lambda b, h: (0, 0)),
            pl.BlockSpec((None, seq, DA_V_DIM), lambda b, h: (b, 0, h)),
            pl.BlockSpec((None, seq, DA_V_DIM), lambda b, h: (b, 0, head_blocks + h)),
            pl.BlockSpec((None, seq, DA_V_DIM), lambda b, h: (b, 0, 2 * head_blocks + h)),
        ],
        out_specs=pl.BlockSpec((None, seq, DA_V_DIM), lambda b, h: (b, 0, h)),
        out_shape=jax.ShapeDtypeStruct((batch, seq, ATTN_WIDTH), BF16),
        scratch_shapes=[
            pltpu.VMEM((seq, DA_V_DIM), BF16),
            pltpu.VMEM((seq, DA_V_DIM), BF16),
            pltpu.VMEM((DA_V_DIM + ONES_ROWS, seq), BF16),
        ],
        compiler_params=pltpu.CompilerParams(
            dimension_semantics=("arbitrary", "arbitrary"), vmem_limit_bytes=VMEM_LIMIT_BYTES),
        name="diff_attention",
    )(da_lam, da_g.reshape(1, DA_V_DIM), qkv, qkv, qkv)


def _gelu_tanh(x):
    c = math.sqrt(2.0 / math.pi)
    return x * (0.5 * (1.0 + jnp.tanh(c * (x + 0.044715 * (x * x * x)))))


def _rec_kernel(xr_ref, gate_ref, cw_ref, cb_ref, wa_ref, wx_ref, ba_ref, bx_ref,
                lam_ref, g_ref, o_ref, xprev_ref, hprev_ref):
    chunk = REC_CHUNK
    c = pl.program_id(1)

    @pl.when(c == 0)
    def _():
        xprev_ref[...] = jnp.zeros(xprev_ref.shape, F32)
        hprev_ref[...] = jnp.zeros(hprev_ref.shape, F32)

    x = xr_ref[...]
    ext = jnp.concatenate([xprev_ref[...], x], axis=0)
    xprev_ref[...] = x[chunk - SUBLANES:, :]
    cw = cw_ref[...]
    conv = cb_ref[...] + x * cw[CONV_WIDTH - 1:CONV_WIDTH, :]
    for tap in range(CONV_WIDTH - 1):
        off = SUBLANES - (CONV_WIDTH - 1) + tap
        conv = conv + ext[off:off + chunk, :] * cw[tap:tap + 1, :]

    xb = conv.astype(BF16)
    half = MXU_DIM

    def block_diag_dot(w_ref):
        lo = jnp.dot(xb[:, :half], w_ref[0], preferred_element_type=F32)
        hi = jnp.dot(xb[:, half:], w_ref[1], preferred_element_type=F32)
        return jnp.concatenate([lo, hi], axis=1)

    r = jax.nn.sigmoid(block_diag_dot(wa_ref) + ba_ref[...])
    gi = jax.nn.sigmoid(block_diag_dot(wx_ref) + bx_ref[...])
    z = -lam_ref[...]
    softplus = jnp.maximum(z, 0.0) + jnp.log1p(jnp.exp(-jnp.abs(z)))
    log_a = -LRU_C * r * softplus
    a = jnp.exp(log_a)
    mult = jnp.sqrt(-jnp.tanh(log_a) * (1.0 + a * a))
    row = lax.broadcasted_iota(jnp.int32, (chunk, 1), 0)
    mult = jnp.where(row + c * chunk == 0, 1.0, mult)
    bt = mult * (gi * conv)

    shift = 1
    while shift < chunk:
        valid = row >= shift
        a_sh = pltpu.roll(a, shift, axis=0)
        b_sh = pltpu.roll(bt, shift, axis=0)
        bt = jnp.where(valid, a * b_sh + bt, bt)
        a = jnp.where(valid, a * a_sh, a)
        shift *= 2
    h = a * hprev_ref[SUBLANES - 1:SUBLANES, :] + bt
    hprev_ref[...] = h[chunk - SUBLANES:, :]

    y = h * _gelu_tanh(gate_ref[...])
    ms = jnp.mean(y * y, axis=-1, keepdims=True)
    o_ref[...] = (y * lax.rsqrt(ms + RMS_EPS) * g_ref[...]).astype(o_ref.dtype)


def _block_diag_halves(w):
    per_tile = MXU_DIM // REC_BLOCK_DIM
    tiles = []
    for t in range(REC_BLOCKS // per_tile):
        tiles.append(jax.scipy.linalg.block_diag(*[w[t * per_tile + n] for n in range(per_tile)]))
    return jnp.stack(tiles).astype(BF16)


def _rec_branch(rec, conv_w, conv_b, w_a, b_a, w_x, b_x, lru_lam, rec_g):
    batch, seq, _ = rec.shape
    chunk = REC_CHUNK
    row_vec = lambda p: p.reshape(1, REC_WIDTH)
    vec_spec = pl.BlockSpec((1, REC_WIDTH), lambda b, c: (0, 0))
    w_spec = pl.BlockSpec((2, MXU_DIM, MXU_DIM), lambda b, c: (0, 0, 0))
    return pl.pallas_call(
        _rec_kernel,
        grid=(batch, seq // chunk),
        in_specs=[
            pl.BlockSpec((None, chunk, REC_WIDTH), lambda b, c: (b, c, 0)),
            pl.BlockSpec((None, chunk, REC_WIDTH), lambda b, c: (b, c, 1)),
            pl.BlockSpec((CONV_WIDTH, REC_WIDTH), lambda b, c: (0, 0)),
            vec_spec, w_spec, w_spec, vec_spec, vec_spec, vec_spec, vec_spec,
        ],
        out_specs=pl.BlockSpec((None, chunk, REC_WIDTH), lambda b, c: (b, c, 0)),
        out_shape=jax.ShapeDtypeStruct((batch, seq, REC_WIDTH), BF16),
        scratch_shapes=[
            pltpu.VMEM((SUBLANES, REC_WIDTH), F32),
            pltpu.VMEM((SUBLANES, REC_WIDTH), F32),
        ],
        compiler_params=pltpu.CompilerParams(
            dimension_semantics=("arbitrary", "arbitrary"), vmem_limit_bytes=VMEM_LIMIT_BYTES),
        name="rg_lru",
    )(rec, rec, conv_w, row_vec(conv_b), _block_diag_halves(w_a), _block_diag_halves(w_x),
      row_vec(b_a), row_vec(b_x), row_vec(lru_lam), row_vec(rec_g))


def _out_ffn_kernel(x_ref, attn_ref, rec_ref, wo_a_ref, wo_r_ref, g1_ref, b1_ref,
                    w1_ref, w2_ref, g2_ref, b2_ref, o_ref, *, alpha):
    for r0 in range(0, FFN_ROWS, FFN_SUB_ROWS):
        rows = slice(r0, r0 + FFN_SUB_ROWS)
        mix = jnp.dot(attn_ref[rows, :], wo_a_ref[...], preferred_element_type=F32)
        mix = mix + jnp.dot(rec_ref[rows, :], wo_r_ref[...], preferred_element_type=F32)
        x1 = _layer_norm(alpha * x_ref[rows, :] + mix, g1_ref[...], b1_ref[...])
        hu = jnp.dot(x1.astype(BF16), w1_ref[...], preferred_element_type=F32)
        g = hu[:, :D_FF]
        u = hu[:, D_FF:]
        act = (g * jax.nn.sigmoid(g) * u).astype(BF16)
        ffn = jnp.dot(act, w2_ref[...], preferred_element_type=F32)
        o_ref[rows, :] = _layer_norm(alpha * x1 + ffn, g2_ref[...], b2_ref[...])


def _out_ffn(x2d, attn2d, rec2d, w_out, ln1_g, ln1_b, w_ffn_in, w_ffn_out, ln2_g, ln2_b, alpha):
    rows = x2d.shape[0]
    tm = FFN_ROWS
    row_vec = lambda p: p.reshape(1, D_MODEL)
    const = lambda shape: pl.BlockSpec(shape, lambda i: (0,) * len(shape),
                                       pipeline_mode=pl.Buffered(1))
    w_out_b = w_out.astype(BF16)
    return pl.pallas_call(
        functools.partial(_out_ffn_kernel, alpha=alpha),
        grid=(rows // tm,),
        in_specs=[
            pl.BlockSpec((tm, D_MODEL), lambda i: (i, 0)),
            pl.BlockSpec((tm, ATTN_WIDTH), lambda i: (i, 0)),
            pl.BlockSpec((tm, REC_WIDTH), lambda i: (i, 0)),
            const((ATTN_WIDTH, D_MODEL)), const((REC_WIDTH, D_MODEL)),
            const((1, D_MODEL)), const((1, D_MODEL)),
            const((D_MODEL, 2 * D_FF)), const((D_FF, D_MODEL)),
            const((1, D_MODEL)), const((1, D_MODEL)),
        ],
        out_specs=pl.BlockSpec((tm, D_MODEL), lambda i: (i, 0)),
        out_shape=jax.ShapeDtypeStruct((rows, D_MODEL), F32),
        compiler_params=pltpu.CompilerParams(
            dimension_semantics=("arbitrary",), vmem_limit_bytes=VMEM_LIMIT_BYTES),
        name="out_ffn",
    )(x2d, attn2d, rec2d, w_out_b[:ATTN_WIDTH], w_out_b[ATTN_WIDTH:], row_vec(ln1_g), row_vec(ln1_b),
      w_ffn_in.astype(BF16), w_ffn_out.astype(BF16), row_vec(ln2_g), row_vec(ln2_b))


def kernel(x, w_in, conv_w, conv_b, lru_w_a, lru_b_a, lru_w_x, lru_b_x, lru_lambda,
           rec_norm_g, da_lambda, da_norm_g, w_out, ln1_g, ln1_b, w_ffn_in, w_ffn_out,
           ln2_g, ln2_b):
    batch, seq, _ = x.shape
    depth = w_in.shape[0]
    alpha = (2.0 * depth) ** 0.25
    x2d = x.reshape(batch * seq, D_MODEL)
    for l in range(depth):
        lambda_init = 0.8 - 0.6 * math.exp(-0.3 * l)
        qkv, rec = _in_proj(x2d, w_in[l].astype(BF16))
        attn = _diff_attention(qkv.reshape(batch, seq, QKV_WIDTH), da_lambda[l], da_norm_g[l],
                               lambda_init)
        rec_out = _rec_branch(rec.reshape(batch, seq, 2 * REC_WIDTH), conv_w[l], conv_b[l],
                              lru_w_a[l], lru_b_a[l], lru_w_x[l], lru_b_x[l], lru_lambda[l],
                              rec_norm_g[l])
        x2d = _out_ffn(x2d, attn.reshape(batch * seq, ATTN_WIDTH),
                       rec_out.reshape(batch * seq, REC_WIDTH), w_out[l], ln1_g[l], ln1_b[l],
                       w_ffn_in[l], w_ffn_out[l], ln2_g[l], ln2_b[l], alpha)
    return x2d.reshape(batch, seq, D_MODEL)
```

```python
import functools
import math

import jax
import jax.numpy as jnp
from jax import lax
from jax.experimental import pallas as pl
from jax.experimental.pallas import tpu as pltpu

D_MODEL = 1024
ATTN_WIDTH = 512
REC_WIDTH = 512
DA_HEAD_DIM = 64
DA_V_DIM = 128
DA_N_HEADS = 4
REC_BLOCKS = 8
REC_BLOCK_DIM = 64
CONV_WIDTH = 4
LRU_C = 8.0
D_FF = 2816
QKV_WIDTH = 3 * ATTN_WIDTH
LN_EPS = 1e-5
RMS_EPS = 1e-5

LANES = 128
SUBLANES = 8
MXU_DIM = 256
VMEM_LIMIT_BYTES = 56 * 1024 * 1024

PROJ_ROWS = 512
ATTN_BLOCK = 256
ONES_ROWS = 16
FFN_ROWS = 512
FFN_SUB_ROWS = 256
FFN_CHUNK = 256
REC_PIECE_ROWS = 64
REC_PIECE_SLOTS = (0, 1, 3, 5, 7, 9)

NEG_BIG = -1e30
LOG2_E = 1.0 / math.log(2.0)
BF16 = jnp.bfloat16
F32 = jnp.float32


def _layer_norm(z, g, b):
    mu = jnp.mean(z, axis=-1, keepdims=True)
    zc = z - mu
    var = jnp.mean(zc * zc, axis=-1, keepdims=True)
    return zc * lax.rsqrt(var + LN_EPS) * g + b


def _in_proj_kernel(x_ref, w_ref, qkv_ref, rec_ref):
    xb = x_ref[...].astype(BF16)
    p = jnp.dot(xb, w_ref[...], preferred_element_type=F32)
    q = p[:, :ATTN_WIDTH] * (DA_HEAD_DIM ** -0.5 * LOG2_E)
    qkv_ref[:, :ATTN_WIDTH] = q.astype(BF16)
    qkv_ref[:, ATTN_WIDTH:] = p[:, ATTN_WIDTH:QKV_WIDTH].astype(BF16)
    rec_ref[...] = p[:, QKV_WIDTH:]


def _in_proj(x2d, w_in_bf16):
    rows = x2d.shape[0]
    in_width = w_in_bf16.shape[1]
    return pl.pallas_call(
        _in_proj_kernel,
        grid=(rows // PROJ_ROWS,),
        in_specs=[
            pl.BlockSpec((PROJ_ROWS, D_MODEL), lambda i: (i, 0)),
            pl.BlockSpec((D_MODEL, in_width), lambda i: (0, 0)),
        ],
        out_specs=[
            pl.BlockSpec((PROJ_ROWS, QKV_WIDTH), lambda i: (i, 0)),
            pl.BlockSpec((PROJ_ROWS, 2 * REC_WIDTH), lambda i: (i, 0)),
        ],
        out_shape=[
            jax.ShapeDtypeStruct((rows, QKV_WIDTH), BF16),
            jax.ShapeDtypeStruct((rows, 2 * REC_WIDTH), F32),
        ],
        compiler_params=pltpu.CompilerParams(
            dimension_semantics=("arbitrary",), vmem_limit_bytes=VMEM_LIMIT_BYTES),
        name="in_proj",
    )(x2d, w_in_bf16)


def _attn_kernel(lam_ref, g_ref, q_ref, k_ref, v_ref, o_ref, ka_ref, kb_ref, vext_ref,
                 *, seq, lambda_init):
    blk = ATTN_BLOCK
    lane = lax.broadcasted_iota(jnp.int32, (1, DA_V_DIM), 1)
    lo = (lane < DA_HEAD_DIM).astype(BF16)
    k_all = k_ref[...]
    ka_ref[...] = k_all * lo
    kb_ref[...] = k_all * (1 - lo)
    vext_ref[:DA_V_DIM, :] = v_ref[...].astype(F32).T.astype(BF16)
    vext_ref[DA_V_DIM:, :] = jnp.ones((ONES_ROWS, seq), BF16)

    lp = lam_ref[...]
    t1 = jnp.sum(lp[0:1, :] * lp[1:2, :], axis=-1, keepdims=True)
    t2 = jnp.sum(lp[2:3, :] * lp[3:4, :], axis=-1, keepdims=True)
    lam = jnp.exp(t1) - jnp.exp(t2) + lambda_init

    kv_idx = lax.broadcasted_iota(jnp.int32, (blk, blk), 0)
    q_idx = lax.broadcasted_iota(jnp.int32, (blk, blk), 1)
    causal = kv_idx <= q_idx
    dims = (((1,), (1,)), ((), ()))

    def scores(i, kx_ref):
        kv_len = (i + 1) * blk
        q = q_ref[i * blk:kv_len, :]
        s = lax.dot_general(kx_ref[:kv_len, :], q, dims, preferred_element_type=F32)
        s_diag = jnp.where(causal, s[kv_len - blk:, :], NEG_BIG)
        if kv_len > blk:
            s = jnp.concatenate([s[:kv_len - blk, :], s_diag], axis=0)
        else:
            s = s_diag
        return s

    def softmax_numerator(s):
        m = jnp.max(s, axis=0, keepdims=True)
        return jnp.exp2(s - m).astype(BF16)

    def weighted_values(i, p):
        kv_len = (i + 1) * blk
        r = jnp.dot(vext_ref[:, :kv_len], p, preferred_element_type=F32)
        return r[:DA_V_DIM, :] / r[DA_V_DIM:DA_V_DIM + 1, :]

    n_blocks = seq // blk
    chains = [(i, kx_ref) for i in reversed(range(n_blocks)) for kx_ref in (ka_ref, kb_ref)]
    n_chains = len(chains)
    s_vals = {0: scores(*chains[0]), 1: scores(*chains[1])}
    p_vals = {0: softmax_numerator(s_vals.pop(0))}
    outs = []
    for c, (i, _) in enumerate(chains):
        if c + 2 < n_chains:
            s_vals[c + 2] = scores(*chains[c + 2])
        if c + 1 < n_chains:
            p_vals[c + 1] = softmax_numerator(s_vals.pop(c + 1))
        outs.append(weighted_values(i, p_vals.pop(c)))
        if len(outs) == 2:
            o_t = outs[0] - lam * outs[1]
            outs = []
            ms = jnp.mean(o_t * o_t, axis=0, keepdims=True)
            o = (o_t * lax.rsqrt(ms + RMS_EPS)).T
            o = o * g_ref[...] * (1.0 - lambda_init)
            o_ref[i * blk:(i + 1) * blk, :] = o.astype(o_ref.dtype)


def _diff_attention(qkv, da_lam, da_g, lambda_init):
    batch, seq, _ = qkv.shape
    kernel = functools.partial(_attn_kernel, seq=seq, lambda_init=lambda_init)
    head_blocks = ATTN_WIDTH // DA_V_DIM
    return pl.pallas_call(
        kernel,
        grid=(batch, DA_N_HEADS),
        in_specs=[
            pl.BlockSpec((4, DA_HEAD_DIM), lambda b, h: (0, 0)),
            pl.BlockSpec((1, DA_V_DIM), lambda b, h: (0, 0)),
            pl.BlockSpec((None, seq, DA_V_DIM), lambda b, h: (b, 0, h)),
            pl.BlockSpec((None, seq, DA_V_DIM), lambda b, h: (b, 0, head_blocks + h)),
            pl.BlockSpec((None, seq, DA_V_DIM), lambda b, h: (b, 0, 2 * head_blocks + h)),
        ],
        out_specs=pl.BlockSpec((None, seq, DA_V_DIM), lambda b, h: (b, 0, h)),
        out_shape=jax.ShapeDtypeStruct((batch, seq, ATTN_WIDTH), BF16),
        scratch_shapes=[
            pltpu.VMEM((seq, DA_V_DIM), BF16),
            pltpu.VMEM((seq, DA_V_DIM), BF16),
            pltpu.VMEM((DA_V_DIM + ONES_ROWS, seq), BF16),
        ],
        compiler_params=pltpu.CompilerParams(
            dimension_semantics=("arbitrary", "arbitrary"), vmem_limit_bytes=VMEM_LIMIT_BYTES),
        name="diff_attention",
    )(da_lam, da_g.reshape(1, DA_V_DIM), qkv, qkv, qkv)


def _gelu_tanh(x):
    c = math.sqrt(2.0 / math.pi)
    return x * (0.5 * (1.0 + jnp.tanh(c * (x + 0.044715 * (x * x * x)))))


def _rec_conv(x, x_tail, cw, cb):
    rows, width = x.shape
    tiles = rows // SUBLANES
    sub = lax.broadcasted_iota(jnp.int32, (1, SUBLANES, 1), 1)
    x3 = x.reshape(tiles, SUBLANES, width)
    conv = cb + x3 * cw[CONV_WIDTH - 1:CONV_WIDTH, :]
    for k in range(1, CONV_WIDTH):
        rot = pltpu.roll(x3, k, axis=1)
        rot_prev = jnp.concatenate([pltpu.roll(x_tail, k, axis=0)[None], rot[:-1]], axis=0)
        tap = CONV_WIDTH - 1 - k
        conv = conv + jnp.where(sub >= k, rot, rot_prev) * cw[tap:tap + 1, :]
    return conv.reshape(rows, width)


def _rec_gate_dots(conv_bf16, wa_ref, wx_ref):
    half = MXU_DIM

    def block_diag_dot(w_ref):
        lo = jnp.dot(conv_bf16[:, :half], w_ref[0], preferred_element_type=F32)
        hi = jnp.dot(conv_bf16[:, half:], w_ref[1], preferred_element_type=F32)
        return jnp.concatenate([lo, hi], axis=1)

    return block_diag_dot(wa_ref), block_diag_dot(wx_ref)


def _rec_scan_rows(conv, ra_pre, gi_pre, gate, h_last, t0, ba, bx, softplus, g):
    rows, width = conv.shape
    tiles = rows // SUBLANES
    sub = lax.broadcasted_iota(jnp.int32, (1, SUBLANES, 1), 1)
    r = jax.nn.sigmoid(ra_pre + ba)
    gi = jax.nn.sigmoid(gi_pre + bx)
    log_a = -LRU_C * r * softplus
    a = jnp.exp(log_a)
    mult = jnp.sqrt(-jnp.tanh(log_a) * (1.0 + a * a))
    row = lax.broadcasted_iota(jnp.int32, (rows, 1), 0)
    mult = jnp.where(row + t0 == 0, 1.0, mult)
    bt = mult * (gi * conv)

    a3 = a.reshape(tiles, SUBLANES, width)
    b3 = bt.reshape(tiles, SUBLANES, width)
    shift = 1
    while shift < SUBLANES:
        valid = sub >= shift
        a_sh = jnp.where(valid, pltpu.roll(a3, shift, axis=1), 1.0)
        b_sh = jnp.where(valid, pltpu.roll(b3, shift, axis=1), 0.0)
        b3 = a3 * b_sh + b3
        a3 = a3 * a_sh
        shift *= 2
    carry = h_last
    h_tiles = []
    for j in range(tiles):
        h_j = a3[j] * carry + b3[j]
        h_tiles.append(h_j)
        carry = h_j[SUBLANES - 1:SUBLANES, :]
    h = jnp.concatenate(h_tiles, axis=0)

    y = h * _gelu_tanh(gate)
    ms = jnp.mean(y * y, axis=-1, keepdims=True)
    return y * lax.rsqrt(ms + RMS_EPS) * g, carry


def _block_diag_halves(w):
    per_tile = MXU_DIM // REC_BLOCK_DIM
    tiles = []
    for t in range(REC_BLOCKS // per_tile):
        tiles.append(jax.scipy.linalg.block_diag(*[w[t * per_tile + n] for n in range(per_tile)]))
    return jnp.stack(tiles).astype(BF16)


def _rec_ffn_kernel(x_ref, attn_ref, xr_ref, gate_ref,
                    cw_ref, cb_ref, wa_ref, wx_ref, ba_ref, bx_ref, lam_ref, rg_ref,
                    wo_a_ref, wo_r_ref, g1_ref, b1_ref, w1_ref, w2_ref, g2_ref, b2_ref,
                    o_ref, xprev_ref, hprev_ref, rec_ref, *, alpha, tiles_per_seq):
    s = pl.program_id(0)

    @pl.when(s == 0)
    def _():
        rec_ref[...] = jnp.zeros(rec_ref.shape, rec_ref.dtype)

    @pl.when(s % tiles_per_seq == 0)
    def _():
        xprev_ref[...] = jnp.zeros(xprev_ref.shape, F32)
        hprev_ref[...] = jnp.zeros(hprev_ref.shape, F32)

    t_base = (s % tiles_per_seq) * FFN_ROWS
    sub_tiles = [slice(r0, r0 + FFN_SUB_ROWS) for r0 in range(0, FFN_ROWS, FFN_SUB_ROWS)]
    n_chunks = D_FF // FFN_CHUNK

    z = -lam_ref[...]
    softplus = jnp.maximum(z, 0.0) + jnp.log1p(jnp.exp(-jnp.abs(z)))
    state = {"x_tail": xprev_ref[...], "h": hprev_ref[0:1, :]}
    gates = {}

    convs = {}

    def conv_piece(rows):
        def piece():
            x = xr_ref[rows, :]
            conv = _rec_conv(x, state["x_tail"], cw_ref[...], cb_ref[...])
            convs[rows.start] = (conv, conv.astype(BF16))
            state["x_tail"] = x[FFN_SUB_ROWS - SUBLANES:, :]
        return piece

    def gate_piece(rows):
        def piece():
            conv, conv_bf16 = convs[rows.start]
            gates[rows.start] = (conv,) + _rec_gate_dots(conv_bf16, wa_ref, wx_ref)
        return piece

    def scan_piece(rows, r0):
        def piece():
            conv, ra_pre, gi_pre = gates[rows.start]
            local = slice(r0 - rows.start, r0 - rows.start + REC_PIECE_ROWS)
            out, state["h"] = _rec_scan_rows(
                conv[local, :], ra_pre[local, :], gi_pre[local, :],
                gate_ref[r0:r0 + REC_PIECE_ROWS, :], state["h"], t_base + r0,
                ba_ref[...], bx_ref[...], softplus, rg_ref[...])
            rec_ref[r0:r0 + REC_PIECE_ROWS, :] = out.astype(rec_ref.dtype)
        return piece

    rec_pieces = {}
    for n, rows in enumerate(sub_tiles):
        pieces = [conv_piece(rows), gate_piece(rows)]
        pieces += [scan_piece(rows, r0) for r0 in range(rows.start, rows.stop, REC_PIECE_ROWS)]
        rec_pieces[n] = pieces

    def project(rows):
        mix = jnp.dot(attn_ref[rows, :], wo_a_ref[...], preferred_element_type=F32)
        return mix + jnp.dot(rec_ref[rows, :], wo_r_ref[...], preferred_element_type=F32)

    def ffn_in(x1b, c):
        cols = slice(c * FFN_CHUNK, (c + 1) * FFN_CHUNK)
        ucols = slice(D_FF + c * FFN_CHUNK, D_FF + (c + 1) * FFN_CHUNK)
        return (jnp.dot(x1b, w1_ref[:, cols], preferred_element_type=F32),
                jnp.dot(x1b, w1_ref[:, ucols], preferred_element_type=F32))

    def ffn_out(gu, c):
        g, u = gu
        act = (g * jax.nn.sigmoid(g) * u).astype(BF16)
        return jnp.dot(act, w2_ref[c * FFN_CHUNK:(c + 1) * FFN_CHUNK, :],
                       preferred_element_type=F32)

    x1 = {0: _layer_norm(alpha * x_ref[sub_tiles[0], :] + project(sub_tiles[0]),
                         g1_ref[...], b1_ref[...])}
    x1b = {0: x1[0].astype(BF16)}
    gu = ffn_in(x1b[0], 0)
    for n, rows in enumerate(sub_tiles):
        last = n + 1 == len(sub_tiles)
        acc = None
        mix_next = None
        for c in range(n_chunks):
            gu_next = None
            if c + 1 < n_chunks:
                gu_next = ffn_in(x1b[n], c + 1)
            elif not last:
                gu_next = ffn_in(x1b[n + 1], 0)
            part = ffn_out(gu, c)
            acc = part if acc is None else acc + part
            gu = gu_next
            if rec_pieces[n] and c in REC_PIECE_SLOTS:
                rec_pieces[n].pop(0)()
            if not last and c == n_chunks - 4:
                mix_next = project(sub_tiles[n + 1])
            if not last and c == n_chunks - 3:
                x1[n + 1] = _layer_norm(alpha * x_ref[sub_tiles[n + 1], :] + mix_next,
                                        g1_ref[...], b1_ref[...])
                x1b[n + 1] = x1[n + 1].astype(BF16)
        o_ref[rows, :] = _layer_norm(alpha * x1[n] + acc, g2_ref[...], b2_ref[...])

    xprev_ref[...] = state["x_tail"]
    hprev_ref[0:1, :] = state["h"]


def _rec_ffn(x2d, attn2d, rec2d, seq, conv_w, conv_b, w_a, b_a, w_x, b_x, lru_lam, rec_g,
             w_out, ln1_g, ln1_b, w_ffn_in, w_ffn_out, ln2_g, ln2_b, alpha):
    rows = x2d.shape[0]
    tm = FFN_ROWS
    n_tiles = rows // tm
    row_vec = lambda p: p.reshape(1, -1)
    const = lambda shape: pl.BlockSpec(shape, lambda s: (0,) * len(shape),
                                       pipeline_mode=pl.Buffered(1))
    ffn_tile = lambda s: (jnp.maximum(s - 1, 0), 0)
    rec_tile = lambda s: jnp.minimum(s, n_tiles - 1)
    w_out_b = w_out.astype(BF16)
    return pl.pallas_call(
        functools.partial(_rec_ffn_kernel, alpha=alpha, tiles_per_seq=seq // tm),
        grid=(n_tiles + 1,),
        in_specs=[
            pl.BlockSpec((tm, D_MODEL), ffn_tile),
            pl.BlockSpec((tm, ATTN_WIDTH), ffn_tile),
            pl.BlockSpec((tm, REC_WIDTH), lambda s: (rec_tile(s), 0)),
            pl.BlockSpec((tm, REC_WIDTH), lambda s: (rec_tile(s), 1)),
            const((CONV_WIDTH, REC_WIDTH)), const((1, REC_WIDTH)),
            const((2, MXU_DIM, MXU_DIM)), const((2, MXU_DIM, MXU_DIM)),
            const((1, REC_WIDTH)), const((1, REC_WIDTH)), const((1, REC_WIDTH)),
            const((1, REC_WIDTH)),
            const((ATTN_WIDTH, D_MODEL)), const((REC_WIDTH, D_MODEL)),
            const((1, D_MODEL)), const((1, D_MODEL)),
            const((D_MODEL, 2 * D_FF)), const((D_FF, D_MODEL)),
            const((1, D_MODEL)), const((1, D_MODEL)),
        ],
        out_specs=pl.BlockSpec((tm, D_MODEL), ffn_tile),
        out_shape=jax.ShapeDtypeStruct((rows, D_MODEL), F32),
        scratch_shapes=[
            pltpu.VMEM((SUBLANES, REC_WIDTH), F32),
            pltpu.VMEM((SUBLANES, REC_WIDTH), F32),
            pltpu.VMEM((tm, REC_WIDTH), BF16),
        ],
        compiler_params=pltpu.CompilerParams(
            dimension_semantics=("arbitrary",), vmem_limit_bytes=VMEM_LIMIT_BYTES),
        name="rec_ffn",
    )(x2d, attn2d, rec2d, rec2d,
      conv_w, row_vec(conv_b), _block_diag_halves(w_a), _block_diag_halves(w_x),
      row_vec(b_a), row_vec(b_x), row_vec(lru_lam), row_vec(rec_g),
      w_out_b[:ATTN_WIDTH], w_out_b[ATTN_WIDTH:], row_vec(ln1_g), row_vec(ln1_b),
      w_ffn_in.astype(BF16), w_ffn_out.astype(BF16), row_vec(ln2_g), row_vec(ln2_b))


def kernel(x, w_in, conv_w, conv_b, lru_w_a, lru_b_a, lru_w_x, lru_b_x, lru_lambda,
           rec_norm_g, da_lambda, da_norm_g, w_out, ln1_g, ln1_b, w_ffn_in, w_ffn_out,
           ln2_g, ln2_b):
    batch, seq, _ = x.shape
    depth = w_in.shape[0]
    alpha = (2.0 * depth) ** 0.25
    x2d = x.reshape(batch * seq, D_MODEL)
    for l in range(depth):
        lambda_init = 0.8 - 0.6 * math.exp(-0.3 * l)
        qkv, rec = _in_proj(x2d, w_in[l].astype(BF16))
        attn = _diff_attention(qkv.reshape(batch, seq, QKV_WIDTH), da_lambda[l], da_norm_g[l],
                               lambda_init)
        x2d = _rec_ffn(x2d, attn.reshape(batch * seq, ATTN_WIDTH), rec, seq,
                       conv_w[l], conv_b[l], lru_w_a[l], lru_b_a[l], lru_w_x[l], lru_b_x[l],
                       lru_lambda[l], rec_norm_g[l], w_out[l], ln1_g[l], ln1_b[l],
                       w_ffn_in[l], w_ffn_out[l], ln2_g[l], ln2_b[l], alpha)
    return x2d.reshape(batch, seq, D_MODEL)
```

```python
import functools
import math

import jax
import jax.numpy as jnp
from jax import lax
from jax.experimental import pallas as pl
from jax.experimental.pallas import tpu as pltpu

D_MODEL = 1024
ATTN_WIDTH = 512
REC_WIDTH = 512
DA_HEAD_DIM = 64
DA_V_DIM = 128
DA_N_HEADS = 4
REC_BLOCKS = 8
REC_BLOCK_DIM = 64
CONV_WIDTH = 4
LRU_C = 8.0
D_FF = 2816
QKV_WIDTH = 3 * ATTN_WIDTH
LN_EPS = 1e-5
RMS_EPS = 1e-5

LANES = 128
SUBLANES = 8
MXU_DIM = 256
VMEM_LIMIT_BYTES = 56 * 1024 * 1024

PROJ_ROWS = 512
ATTN_BLOCK = 256
ONES_ROWS = 16
ATTN_HEADS_PER_STEP = 2
ATTN_SCORE_LOOKAHEAD = 4
FFN_ROWS = 512
FFN_SUB_ROWS = 256
FFN_CHUNK = 256
REC_PIECE_ROWS = 64
REC_PIECE_SLOTS = (0, 1, 3, 5, 7, 9)

NEG_BIG = -1e30
LOG2_E = 1.0 / math.log(2.0)
BF16 = jnp.bfloat16
F32 = jnp.float32


def _layer_norm(z, g, b):
    mu = jnp.mean(z, axis=-1, keepdims=True)
    zc = z - mu
    var = jnp.mean(zc * zc, axis=-1, keepdims=True)
    return zc * lax.rsqrt(var + LN_EPS) * g + b


def _in_proj_kernel(x_ref, w_ref, qkv_ref, rec_ref):
    xb = x_ref[...].astype(BF16)
    p = jnp.dot(xb, w_ref[...], preferred_element_type=F32)
    q = p[:, :ATTN_WIDTH] * (DA_HEAD_DIM ** -0.5 * LOG2_E)
    qkv_ref[:, :ATTN_WIDTH] = q.astype(BF16)
    qkv_ref[:, ATTN_WIDTH:] = p[:, ATTN_WIDTH:QKV_WIDTH].astype(BF16)
    rec_ref[...] = p[:, QKV_WIDTH:]


def _in_proj(x2d, w_in_bf16):
    rows = x2d.shape[0]
    in_width = w_in_bf16.shape[1]
    return pl.pallas_call(
        _in_proj_kernel,
        grid=(rows // PROJ_ROWS,),
        in_specs=[
            pl.BlockSpec((PROJ_ROWS, D_MODEL), lambda i: (i, 0)),
            pl.BlockSpec((D_MODEL, in_width), lambda i: (0, 0)),
        ],
        out_specs=[
            pl.BlockSpec((PROJ_ROWS, QKV_WIDTH), lambda i: (i, 0)),
            pl.BlockSpec((PROJ_ROWS, 2 * REC_WIDTH), lambda i: (i, 0)),
        ],
        out_shape=[
            jax.ShapeDtypeStruct((rows, QKV_WIDTH), BF16),
            jax.ShapeDtypeStruct((rows, 2 * REC_WIDTH), F32),
        ],
        compiler_params=pltpu.CompilerParams(
            dimension_semantics=("arbitrary",), vmem_limit_bytes=VMEM_LIMIT_BYTES),
        name="in_proj",
    )(x2d, w_in_bf16)


def _attn_kernel(lam_ref, g_ref, q_ref, k_ref, v_ref, o_ref, ka_ref, kb_ref, vext_ref,
                 *, seq, lambda_init):
    blk = ATTN_BLOCK
    lane = lax.broadcasted_iota(jnp.int32, (1, DA_V_DIM), 1)
    lo = (lane < DA_HEAD_DIM).astype(BF16)
    for hh in range(ATTN_HEADS_PER_STEP):
        cols = slice(hh * DA_V_DIM, (hh + 1) * DA_V_DIM)
        k_all = k_ref[:, cols]
        ka_ref[hh] = k_all * lo
        kb_ref[hh] = k_all * (1 - lo)
        vext_ref[hh, :DA_V_DIM, :] = v_ref[:, cols].astype(F32).T.astype(BF16)
        vext_ref[hh, DA_V_DIM:, :] = jnp.ones((ONES_ROWS, seq), BF16)

    lp = lam_ref[...]
    t1 = jnp.sum(lp[0:1, :] * lp[1:2, :], axis=-1, keepdims=True)
    t2 = jnp.sum(lp[2:3, :] * lp[3:4, :], axis=-1, keepdims=True)
    lam = jnp.exp(t1) - jnp.exp(t2) + lambda_init

    kv_idx = lax.broadcasted_iota(jnp.int32, (blk, blk), 0)
    q_idx = lax.broadcasted_iota(jnp.int32, (blk, blk), 1)
    causal = kv_idx <= q_idx
    dims = (((1,), (1,)), ((), ()))

    def scores(hh, i, kx_ref):
        kv_len = (i + 1) * blk
        q = q_ref[i * blk:kv_len, hh * DA_V_DIM:(hh + 1) * DA_V_DIM]
        s = lax.dot_general(kx_ref[hh, :kv_len, :], q, dims, preferred_element_type=F32)
        s_diag = jnp.where(causal, s[kv_len - blk:, :], NEG_BIG)
        if kv_len > blk:
            s = jnp.concatenate([s[:kv_len - blk, :], s_diag], axis=0)
        else:
            s = s_diag
        return s

    def softmax_numerator(s):
        m = jnp.max(s, axis=0, keepdims=True)
        return jnp.exp2(s - m).astype(BF16)

    def weighted_values(hh, i, p):
        kv_len = (i + 1) * blk
        r = jnp.dot(vext_ref[hh, :, :kv_len], p, preferred_element_type=F32)
        return r[:DA_V_DIM, :] / r[DA_V_DIM:DA_V_DIM + 1, :]

    n_blocks = seq // blk
    chains = [(hh, i, kx_ref) for i in reversed(range(n_blocks))
              for hh in range(ATTN_HEADS_PER_STEP) for kx_ref in (ka_ref, kb_ref)]
    n_chains = len(chains)
    ahead = ATTN_SCORE_LOOKAHEAD
    s_vals = {c: scores(*chains[c]) for c in range(ahead)}
    p_vals = {0: softmax_numerator(s_vals.pop(0))}
    outs = []
    for c, (hh, i, _) in enumerate(chains):
        if c + ahead < n_chains:
            s_vals[c + ahead] = scores(*chains[c + ahead])
        if c + 1 < n_chains:
            p_vals[c + 1] = softmax_numerator(s_vals.pop(c + 1))
        outs.append(weighted_values(hh, i, p_vals.pop(c)))
        if len(outs) == 2:
            o_t = outs[0] - lam * outs[1]
            outs = []
            ms = jnp.mean(o_t * o_t, axis=0, keepdims=True)
            o = (o_t * lax.rsqrt(ms + RMS_EPS)).T
            o = o * g_ref[...] * (1.0 - lambda_init)
            o_ref[i * blk:(i + 1) * blk, hh * DA_V_DIM:(hh + 1) * DA_V_DIM] = o.astype(o_ref.dtype)


def _diff_attention(qkv, da_lam, da_g, lambda_init):
    batch, seq, _ = qkv.shape
    kernel = functools.partial(_attn_kernel, seq=seq, lambda_init=lambda_init)
    hp = ATTN_HEADS_PER_STEP
    width = hp * DA_V_DIM
    groups = ATTN_WIDTH // width
    return pl.pallas_call(
        kernel,
        grid=(batch, groups),
        in_specs=[
            pl.BlockSpec((4, DA_HEAD_DIM), lambda b, h: (0, 0)),
            pl.BlockSpec((1, DA_V_DIM), lambda b, h: (0, 0)),
            pl.BlockSpec((None, seq, width), lambda b, h: (b, 0, h)),
            pl.BlockSpec((None, seq, width), lambda b, h: (b, 0, groups + h)),
            pl.BlockSpec((None, seq, width), lambda b, h: (b, 0, 2 * groups + h)),
        ],
        out_specs=pl.BlockSpec((None, seq, width), lambda b, h: (b, 0, h)),
        out_shape=jax.ShapeDtypeStruct((batch, seq, ATTN_WIDTH), BF16),
        scratch_shapes=[
            pltpu.VMEM((hp, seq, DA_V_DIM), BF16),
            pltpu.VMEM((hp, seq, DA_V_DIM), BF16),
            pltpu.VMEM((hp, DA_V_DIM + ONES_ROWS, seq), BF16),
        ],
        compiler_params=pltpu.CompilerParams(
            dimension_semantics=("arbitrary", "arbitrary"), vmem_limit_bytes=VMEM_LIMIT_BYTES),
        name="diff_attention",
    )(da_lam, da_g.reshape(1, DA_V_DIM), qkv, qkv, qkv)


def _gelu_tanh(x):
    c = math.sqrt(2.0 / math.pi)
    return x * (0.5 * (1.0 + jnp.tanh(c * (x + 0.044715 * (x * x * x)))))


def _rec_conv(x, x_tail, cw, cb):
    rows, width = x.shape
    tiles = rows // SUBLANES
    sub = lax.broadcasted_iota(jnp.int32, (1, SUBLANES, 1), 1)
    x3 = x.reshape(tiles, SUBLANES, width)
    conv = cb + x3 * cw[CONV_WIDTH - 1:CONV_WIDTH, :]
    for k in range(1, CONV_WIDTH):
        rot = pltpu.roll(x3, k, axis=1)
        rot_prev = jnp.concatenate([pltpu.roll(x_tail, k, axis=0)[None], rot[:-1]], axis=0)
        tap = CONV_WIDTH - 1 - k
        conv = conv + jnp.where(sub >= k, rot, rot_prev) * cw[tap:tap + 1, :]
    return conv.reshape(rows, width)


def _rec_gate_dots(conv_bf16, wa_ref, wx_ref):
    half = MXU_DIM

    def block_diag_dot(w_ref):
        lo = jnp.dot(conv_bf16[:, :half], w_ref[0], preferred_element_type=F32)
        hi = jnp.dot(conv_bf16[:, half:], w_ref[1], preferred_element_type=F32)
        return jnp.concatenate([lo, hi], axis=1)

    return block_diag_dot(wa_ref), block_diag_dot(wx_ref)


def _rec_scan_rows(conv, ra_pre, gi_pre, gate, h_last, t0, ba, bx, softplus, g):
    rows, width = conv.shape
    tiles = rows // SUBLANES
    sub = lax.broadcasted_iota(jnp.int32, (1, SUBLANES, 1), 1)
    r = jax.nn.sigmoid(ra_pre + ba)
    gi = jax.nn.sigmoid(gi_pre + bx)
    log_a = -LRU_C * r * softplus
    a = jnp.exp(log_a)
    mult = jnp.sqrt(-jnp.tanh(log_a) * (1.0 + a * a))
    row = lax.broadcasted_iota(jnp.int32, (rows, 1), 0)
    mult = jnp.where(row + t0 == 0, 1.0, mult)
    bt = mult * (gi * conv)

    a3 = a.reshape(tiles, SUBLANES, width)
    b3 = bt.reshape(tiles, SUBLANES, width)
    shift = 1
    while shift < SUBLANES:
        valid = sub >= shift
        a_sh = jnp.where(valid, pltpu.roll(a3, shift, axis=1), 1.0)
        b_sh = jnp.where(valid, pltpu.roll(b3, shift, axis=1), 0.0)
        b3 = a3 * b_sh + b3
        a3 = a3 * a_sh
        shift *= 2
    carry = h_last
    h_tiles = []
    for j in range(tiles):
        h_j = a3[j] * carry + b3[j]
        h_tiles.append(h_j)
        carry = h_j[SUBLANES - 1:SUBLANES, :]
    h = jnp.concatenate(h_tiles, axis=0)

    y = h * _gelu_tanh(gate)
    ms = jnp.mean(y * y, axis=-1, keepdims=True)
    return y * lax.rsqrt(ms + RMS_EPS) * g, carry


def _block_diag_halves(w):
    per_tile = MXU_DIM // REC_BLOCK_DIM
    tiles = []
    for t in range(REC_BLOCKS // per_tile):
        tiles.append(jax.scipy.linalg.block_diag(*[w[t * per_tile + n] for n in range(per_tile)]))
    return jnp.stack(tiles).astype(BF16)


def _rec_ffn_kernel(x_ref, attn_ref, xr_ref, gate_ref,
                    cw_ref, cb_ref, wa_ref, wx_ref, ba_ref, bx_ref, lam_ref, rg_ref,
                    wo_a_ref, wo_r_ref, g1_ref, b1_ref, w1_ref, w2_ref, g2_ref, b2_ref,
                    o_ref, xprev_ref, hprev_ref, rec_ref, *, alpha, tiles_per_seq):
    s = pl.program_id(0)

    @pl.when(s == 0)
    def _():
        rec_ref[...] = jnp.zeros(rec_ref.shape, rec_ref.dtype)

    @pl.when(s % tiles_per_seq == 0)
    def _():
        xprev_ref[...] = jnp.zeros(xprev_ref.shape, F32)
        hprev_ref[...] = jnp.zeros(hprev_ref.shape, F32)

    t_base = (s % tiles_per_seq) * FFN_ROWS
    sub_tiles = [slice(r0, r0 + FFN_SUB_ROWS) for r0 in range(0, FFN_ROWS, FFN_SUB_ROWS)]
    n_chunks = D_FF // FFN_CHUNK

    z = -lam_ref[...]
    softplus = jnp.maximum(z, 0.0) + jnp.log1p(jnp.exp(-jnp.abs(z)))
    state = {"x_tail": xprev_ref[...], "h": hprev_ref[0:1, :]}
    gates = {}

    convs = {}

    def conv_piece(rows):
        def piece():
            x = xr_ref[rows, :]
            conv = _rec_conv(x, state["x_tail"], cw_ref[...], cb_ref[...])
            convs[rows.start] = (conv, conv.astype(BF16))
            state["x_tail"] = x[FFN_SUB_ROWS - SUBLANES:, :]
        return piece

    def gate_piece(rows):
        def piece():
            conv, conv_bf16 = convs[rows.start]
            gates[rows.start] = (conv,) + _rec_gate_dots(conv_bf16, wa_ref, wx_ref)
        return piece

    def scan_piece(rows, r0):
        def piece():
            conv, ra_pre, gi_pre = gates[rows.start]
            local = slice(r0 - rows.start, r0 - rows.start + REC_PIECE_ROWS)
            out, state["h"] = _rec_scan_rows(
                conv[local, :], ra_pre[local, :], gi_pre[local, :],
                gate_ref[r0:r0 + REC_PIECE_ROWS, :], state["h"], t_base + r0,
                ba_ref[...], bx_ref[...], softplus, rg_ref[...])
            rec_ref[r0:r0 + REC_PIECE_ROWS, :] = out.astype(rec_ref.dtype)
        return piece

    rec_pieces = {}
    for n, rows in enumerate(sub_tiles):
        pieces = [conv_piece(rows), gate_piece(rows)]
        pieces += [scan_piece(rows, r0) for r0 in range(rows.start, rows.stop, REC_PIECE_ROWS)]
        rec_pieces[n] = pieces

    def project(rows):
        mix = jnp.dot(attn_ref[rows, :], wo_a_ref[...], preferred_element_type=F32)
        return mix + jnp.dot(rec_ref[rows, :], wo_r_ref[...], preferred_element_type=F32)

    def ffn_in(x1b, c):
        cols = slice(c * FFN_CHUNK, (c + 1) * FFN_CHUNK)
        ucols = slice(D_FF + c * FFN_CHUNK, D_FF + (c + 1) * FFN_CHUNK)
        return (jnp.dot(x1b, w1_ref[:, cols], preferred_element_type=F32),
                jnp.dot(x1b, w1_ref[:, ucols], preferred_element_type=F32))

    def ffn_out(gu, c):
        g, u = gu
        act = (g * jax.nn.sigmoid(g) * u).astype(BF16)
        return jnp.dot(act, w2_ref[c * FFN_CHUNK:(c + 1) * FFN_CHUNK, :],
                       preferred_element_type=F32)

    x1 = {0: _layer_norm(alpha * x_ref[sub_tiles[0], :] + project(sub_tiles[0]),
                         g1_ref[...], b1_ref[...])}
    x1b = {0: x1[0].astype(BF16)}
    gu = ffn_in(x1b[0], 0)
    for n, rows in enumerate(sub_tiles):
        last = n + 1 == len(sub_tiles)
        acc = None
        mix_next = None
        for c in range(n_chunks):
            gu_next = None
            if c + 1 < n_chunks:
                gu_next = ffn_in(x1b[n], c + 1)
            elif not last:
                gu_next = ffn_in(x1b[n + 1], 0)
            part = ffn_out(gu, c)
            acc = part if acc is None else acc + part
            gu = gu_next
            if rec_pieces[n] and c in REC_PIECE_SLOTS:
                rec_pieces[n].pop(0)()
            if not last and c == n_chunks - 4:
                mix_next = project(sub_tiles[n + 1])
            if not last and c == n_chunks - 3:
                x1[n + 1] = _layer_norm(alpha * x_ref[sub_tiles[n + 1], :] + mix_next,
                                        g1_ref[...], b1_ref[...])
                x1b[n + 1] = x1[n + 1].astype(BF16)
        o_ref[rows, :] = _layer_norm(alpha * x1[n] + acc, g2_ref[...], b2_ref[...])

    xprev_ref[...] = state["x_tail"]
    hprev_ref[0:1, :] = state["h"]


def _rec_ffn(x2d, attn2d, rec2d, seq, conv_w, conv_b, w_a, b_a, w_x, b_x, lru_lam, rec_g,
             w_out, ln1_g, ln1_b, w_ffn_in, w_ffn_out, ln2_g, ln2_b, alpha):
    rows = x2d.shape[0]
    tm = FFN_ROWS
    n_tiles = rows // tm
    row_vec = lambda p: p.reshape(1, -1)
    const = lambda shape: pl.BlockSpec(shape, lambda s: (0,) * len(shape),
                                       pipeline_mode=pl.Buffered(1))
    ffn_tile = lambda s: (jnp.maximum(s - 1, 0), 0)
    rec_tile = lambda s: jnp.minimum(s, n_tiles - 1)
    w_out_b = w_out.astype(BF16)
    return pl.pallas_call(
        functools.partial(_rec_ffn_kernel, alpha=alpha, tiles_per_seq=seq // tm),
        grid=(n_tiles + 1,),
        in_specs=[
            pl.BlockSpec((tm, D_MODEL), ffn_tile),
            pl.BlockSpec((tm, ATTN_WIDTH), ffn_tile),
            pl.BlockSpec((tm, REC_WIDTH), lambda s: (rec_tile(s), 0)),
            pl.BlockSpec((tm, REC_WIDTH), lambda s: (rec_tile(s), 1)),
            const((CONV_WIDTH, REC_WIDTH)), const((1, REC_WIDTH)),
            const((2, MXU_DIM, MXU_DIM)), const((2, MXU_DIM, MXU_DIM)),
            const((1, REC_WIDTH)), const((1, REC_WIDTH)), const((1, REC_WIDTH)),
            const((1, REC_WIDTH)),
            const((ATTN_WIDTH, D_MODEL)), const((REC_WIDTH, D_MODEL)),
            const((1, D_MODEL)), const((1, D_MODEL)),
            const((D_MODEL, 2 * D_FF)), const((D_FF, D_MODEL)),
            const((1, D_MODEL)), const((1, D_MODEL)),
        ],
        out_specs=pl.BlockSpec((tm, D_MODEL), ffn_tile),
        out_shape=jax.ShapeDtypeStruct((rows, D_MODEL), F32),
        scratch_shapes=[
            pltpu.VMEM((SUBLANES, REC_WIDTH), F32),
            pltpu.VMEM((SUBLANES, REC_WIDTH), F32),
            pltpu.VMEM((tm, REC_WIDTH), BF16),
        ],
        compiler_params=pltpu.CompilerParams(
            dimension_semantics=("arbitrary",), vmem_limit_bytes=VMEM_LIMIT_BYTES),
        name="rec_ffn",
    )(x2d, attn2d, rec2d, rec2d,
      conv_w, row_vec(conv_b), _block_diag_halves(w_a), _block_diag_halves(w_x),
      row_vec(b_a), row_vec(b_x), row_vec(lru_lam), row_vec(rec_g),
      w_out_b[:ATTN_WIDTH], w_out_b[ATTN_WIDTH:], row_vec(ln1_g), row_vec(ln1_b),
      w_ffn_in.astype(BF16), w_ffn_out.astype(BF16), row_vec(ln2_g), row_vec(ln2_b))


def kernel(x, w_in, conv_w, conv_b, lru_w_a, lru_b_a, lru_w_x, lru_b_x, lru_lambda,
           rec_norm_g, da_lambda, da_norm_g, w_out, ln1_g, ln1_b, w_ffn_in, w_ffn_out,
           ln2_g, ln2_b):
    batch, seq, _ = x.shape
    depth = w_in.shape[0]
    alpha = (2.0 * depth) ** 0.25
    x2d = x.reshape(batch * seq, D_MODEL)
    for l in range(depth):
        lambda_init = 0.8 - 0.6 * math.exp(-0.3 * l)
        qkv, rec = _in_proj(x2d, w_in[l].astype(BF16))
        attn = _diff_attention(qkv.reshape(batch, seq, QKV_WIDTH), da_lambda[l], da_norm_g[l],
                               lambda_init)
        x2d = _rec_ffn(x2d, attn.reshape(batch * seq, ATTN_WIDTH), rec, seq,
                       conv_w[l], conv_b[l], lru_w_a[l], lru_b_a[l], lru_w_x[l], lru_b_x[l],
                       lru_lambda[l], rec_norm_g[l], w_out[l], ln1_g[l], ln1_b[l],
                       w_ffn_in[l], w_ffn_out[l], ln2_g[l], ln2_b[l], alpha)
    return x2d.reshape(batch, seq, D_MODEL)
```

```python
import functools
import math

import jax
import jax.numpy as jnp
from jax import lax
from jax.experimental import pallas as pl
from jax.experimental.pallas import tpu as pltpu

D_MODEL = 1024
ATTN_WIDTH = 512
REC_WIDTH = 512
DA_HEAD_DIM = 64
DA_V_DIM = 128
DA_N_HEADS = 4
REC_BLOCKS = 8
REC_BLOCK_DIM = 64
CONV_WIDTH = 4
LRU_C = 8.0
D_FF = 2816
QKV_WIDTH = 3 * ATTN_WIDTH
LN_EPS = 1e-5
RMS_EPS = 1e-5

LANES = 128
SUBLANES = 8
MXU_DIM = 256
VMEM_LIMIT_BYTES = 56 * 1024 * 1024

PROJ_ROWS = 512
ATTN_BLOCK = 256
ONES_ROWS = 16
ATTN_HEADS_PER_STEP = 2
ATTN_SCORE_LOOKAHEAD = 4
FFN_ROWS = 512
FFN_SUB_ROWS = 256
FFN_CHUNK = 256
REC_PIECE_ROWS = 64

NEG_BIG = -1e30
LOG2_E = 1.0 / math.log(2.0)
BF16 = jnp.bfloat16
F32 = jnp.float32


def _layer_norm(z, g, b):
    mu = jnp.mean(z, axis=-1, keepdims=True)
    zc = z - mu
    var = jnp.mean(zc * zc, axis=-1, keepdims=True)
    return zc * lax.rsqrt(var + LN_EPS) * g + b


def _in_proj_kernel(x_ref, w_ref, *refs):
    n_cast = (len(refs) - 2) // 2
    cast_in, (qkv_ref, rec_ref), cast_out = refs[:n_cast], refs[n_cast:n_cast + 2], refs[n_cast + 2:]
    xb = x_ref[...].astype(BF16)
    p = jnp.dot(xb, w_ref[...], preferred_element_type=F32)
    q = p[:, :ATTN_WIDTH] * (DA_HEAD_DIM ** -0.5 * LOG2_E)
    qkv_ref[:, :ATTN_WIDTH] = q.astype(BF16)
    qkv_ref[:, ATTN_WIDTH:] = p[:, ATTN_WIDTH:QKV_WIDTH].astype(BF16)
    rec_ref[...] = p[:, QKV_WIDTH:]
    for src_ref, dst_ref in zip(cast_in, cast_out):
        dst_ref[...] = src_ref[...].astype(dst_ref.dtype)


def _cast_view(w, n_steps):
    rows_per_tile = 2 * SUBLANES
    size = w.size
    for cols in (w.shape[-1], 2 * LANES, LANES):
        if size % cols == 0 and (size // cols) % (n_steps * rows_per_tile) == 0:
            return w.reshape(size // cols, cols)
    raise ValueError(f"no slab view for weight of shape {w.shape}")


def _in_proj(x2d, w_in_bf16, later_weights):
    rows = x2d.shape[0]
    in_width = w_in_bf16.shape[1]
    n_steps = rows // PROJ_ROWS
    views = [_cast_view(w, n_steps) for w in later_weights]
    slab_specs = [pl.BlockSpec((v.shape[0] // n_steps, v.shape[1]), lambda i: (i, 0))
                  for v in views]
    outs = pl.pallas_call(
        _in_proj_kernel,
        grid=(n_steps,),
        in_specs=[
            pl.BlockSpec((PROJ_ROWS, D_MODEL), lambda i: (i, 0)),
            pl.BlockSpec((D_MODEL, in_width), lambda i: (0, 0)),
        ] + slab_specs,
        out_specs=[
            pl.BlockSpec((PROJ_ROWS, QKV_WIDTH), lambda i: (i, 0)),
            pl.BlockSpec((PROJ_ROWS, 2 * REC_WIDTH), lambda i: (i, 0)),
        ] + slab_specs,
        out_shape=[
            jax.ShapeDtypeStruct((rows, QKV_WIDTH), BF16),
            jax.ShapeDtypeStruct((rows, 2 * REC_WIDTH), F32),
        ] + [jax.ShapeDtypeStruct(v.shape, BF16) for v in views],
        compiler_params=pltpu.CompilerParams(
            dimension_semantics=("arbitrary",), vmem_limit_bytes=VMEM_LIMIT_BYTES),
        name="in_proj",
    )(x2d, w_in_bf16, *views)
    casts = [c.reshape(w.shape) for c, w in zip(outs[2:], later_weights)]
    return outs[0], outs[1], casts


def _attn_kernel(lam_ref, g_ref, q_ref, k_ref, v_ref, o_ref, ka_ref, kb_ref, vext_ref,
                 *, seq, lambda_init):
    blk = ATTN_BLOCK
    lane = lax.broadcasted_iota(jnp.int32, (1, DA_V_DIM), 1)
    lo = (lane < DA_HEAD_DIM).astype(BF16)
    for hh in range(ATTN_HEADS_PER_STEP):
        cols = slice(hh * DA_V_DIM, (hh + 1) * DA_V_DIM)
        k_all = k_ref[:, cols]
        ka_ref[hh] = k_all * lo
        kb_ref[hh] = k_all * (1 - lo)
        vext_ref[hh, :DA_V_DIM, :] = v_ref[:, cols].astype(F32).T.astype(BF16)
        vext_ref[hh, DA_V_DIM:, :] = jnp.ones((ONES_ROWS, seq), BF16)

    lp = lam_ref[...]
    t1 = jnp.sum(lp[0:1, :] * lp[1:2, :], axis=-1, keepdims=True)
    t2 = jnp.sum(lp[2:3, :] * lp[3:4, :], axis=-1, keepdims=True)
    lam = jnp.exp(t1) - jnp.exp(t2) + lambda_init

    kv_idx = lax.broadcasted_iota(jnp.int32, (blk, blk), 0)
    q_idx = lax.broadcasted_iota(jnp.int32, (blk, blk), 1)
    causal = kv_idx <= q_idx
    dims = (((1,), (1,)), ((), ()))

    def scores(hh, i, kx_ref):
        kv_len = (i + 1) * blk
        q = q_ref[i * blk:kv_len, hh * DA_V_DIM:(hh + 1) * DA_V_DIM]
        s = lax.dot_general(kx_ref[hh, :kv_len, :], q, dims, preferred_element_type=F32)
        s_diag = jnp.where(causal, s[kv_len - blk:, :], NEG_BIG)
        if kv_len > blk:
            s = jnp.concatenate([s[:kv_len - blk, :], s_diag], axis=0)
        else:
            s = s_diag
        return s

    def softmax_numerator(s):
        m = jnp.max(s, axis=0, keepdims=True)
        return jnp.exp2(s - m).astype(BF16)

    def weighted_values(hh, i, p):
        kv_len = (i + 1) * blk
        r = jnp.dot(vext_ref[hh, :, :kv_len], p, preferred_element_type=F32)
        return r[:DA_V_DIM, :] / r[DA_V_DIM:DA_V_DIM + 1, :]

    n_blocks = seq // blk
    chains = [(hh, i, kx_ref) for i in reversed(range(n_blocks))
              for hh in range(ATTN_HEADS_PER_STEP) for kx_ref in (ka_ref, kb_ref)]
    n_chains = len(chains)
    ahead = ATTN_SCORE_LOOKAHEAD
    s_vals = {c: scores(*chains[c]) for c in range(ahead)}
    p_vals = {0: softmax_numerator(s_vals.pop(0))}
    outs = []
    for c, (hh, i, _) in enumerate(chains):
        if c + ahead < n_chains:
            s_vals[c + ahead] = scores(*chains[c + ahead])
        if c + 1 < n_chains:
            p_vals[c + 1] = softmax_numerator(s_vals.pop(c + 1))
        outs.append(weighted_values(hh, i, p_vals.pop(c)))
        if len(outs) == 2:
            o_t = outs[0] - lam * outs[1]
            outs = []
            ms = jnp.mean(o_t * o_t, axis=0, keepdims=True)
            o = (o_t * lax.rsqrt(ms + RMS_EPS)).T
            o = o * g_ref[...] * (1.0 - lambda_init)
            o_ref[i * blk:(i + 1) * blk, hh * DA_V_DIM:(hh + 1) * DA_V_DIM] = o.astype(o_ref.dtype)


def _diff_attention(qkv, da_lam, da_g, lambda_init):
    batch, seq, _ = qkv.shape
    kernel = functools.partial(_attn_kernel, seq=seq, lambda_init=lambda_init)
    hp = ATTN_HEADS_PER_STEP
    width = hp * DA_V_DIM
    groups = ATTN_WIDTH // width
    return pl.pallas_call(
        kernel,
        grid=(batch, groups),
        in_specs=[
            pl.BlockSpec((4, DA_HEAD_DIM), lambda b, h: (0, 0)),
            pl.BlockSpec((1, DA_V_DIM), lambda b, h: (0, 0)),
            pl.BlockSpec((None, seq, width), lambda b, h: (b, 0, h)),
            pl.BlockSpec((None, seq, width), lambda b, h: (b, 0, groups + h)),
            pl.BlockSpec((None, seq, width), lambda b, h: (b, 0, 2 * groups + h)),
        ],
        out_specs=pl.BlockSpec((None, seq, width), lambda b, h: (b, 0, h)),
        out_shape=jax.ShapeDtypeStruct((batch, seq, ATTN_WIDTH), BF16),
        scratch_shapes=[
            pltpu.VMEM((hp, seq, DA_V_DIM), BF16),
            pltpu.VMEM((hp, seq, DA_V_DIM), BF16),
            pltpu.VMEM((hp, DA_V_DIM + ONES_ROWS, seq), BF16),
        ],
        compiler_params=pltpu.CompilerParams(
            dimension_semantics=("arbitrary", "arbitrary"), vmem_limit_bytes=VMEM_LIMIT_BYTES),
        name="diff_attention",
    )(da_lam, da_g.reshape(1, DA_V_DIM), qkv, qkv, qkv)


def _gelu_tanh(x):
    c = math.sqrt(2.0 / math.pi)
    return x * (0.5 * (1.0 + jnp.tanh(c * (x + 0.044715 * (x * x * x)))))


def _rec_conv(x, x_tail, cw, cb):
    rows, width = x.shape
    tiles = rows // SUBLANES
    sub = lax.broadcasted_iota(jnp.int32, (1, SUBLANES, 1), 1)
    x3 = x.reshape(tiles, SUBLANES, width)
    conv = cb + x3 * cw[CONV_WIDTH - 1:CONV_WIDTH, :]
    for k in range(1, CONV_WIDTH):
        rot = pltpu.roll(x3, k, axis=1)
        rot_prev = jnp.concatenate([pltpu.roll(x_tail, k, axis=0)[None], rot[:-1]], axis=0)
        tap = CONV_WIDTH - 1 - k
        conv = conv + jnp.where(sub >= k, rot, rot_prev) * cw[tap:tap + 1, :]
    return conv.reshape(rows, width)


def _rec_gate_dots(conv_bf16, wa_ref, wx_ref):
    half = MXU_DIM

    def block_diag_dot(w_ref):
        lo = jnp.dot(conv_bf16[:, :half], w_ref[0], preferred_element_type=F32)
        hi = jnp.dot(conv_bf16[:, half:], w_ref[1], preferred_element_type=F32)
        return jnp.concatenate([lo, hi], axis=1)

    return block_diag_dot(wa_ref), block_diag_dot(wx_ref)


def _rec_scan_rows(conv, ra_pre, gi_pre, gate, h_last, t0, ba, bx, softplus, g):
    rows, width = conv.shape
    tiles = rows // SUBLANES
    sub = lax.broadcasted_iota(jnp.int32, (1, SUBLANES, 1), 1)
    r = jax.nn.sigmoid(ra_pre + ba)
    gi = jax.nn.sigmoid(gi_pre + bx)
    log_a = -LRU_C * r * softplus
    a = jnp.exp(log_a)
    mult = jnp.sqrt(-jnp.tanh(log_a) * (1.0 + a * a))
    row = lax.broadcasted_iota(jnp.int32, (rows, 1), 0)
    mult = jnp.where(row + t0 == 0, 1.0, mult)
    bt = mult * (gi * conv)

    a3 = a.reshape(tiles, SUBLANES, width)
    b3 = bt.reshape(tiles, SUBLANES, width)
    shift = 1
    while shift < SUBLANES:
        valid = sub >= shift
        a_sh = jnp.where(valid, pltpu.roll(a3, shift, axis=1), 1.0)
        b_sh = jnp.where(valid, pltpu.roll(b3, shift, axis=1), 0.0)
        b3 = a3 * b_sh + b3
        a3 = a3 * a_sh
        shift *= 2
    carry = h_last
    h_tiles = []
    for j in range(tiles):
        h_j = a3[j] * carry + b3[j]
        h_tiles.append(h_j)
        carry = h_j[SUBLANES - 1:SUBLANES, :]
    h = jnp.concatenate(h_tiles, axis=0)

    y = h * _gelu_tanh(gate)
    ms = jnp.mean(y * y, axis=-1, keepdims=True)
    return y * lax.rsqrt(ms + RMS_EPS) * g, carry


def _block_diag_halves(w):
    per_tile = MXU_DIM // REC_BLOCK_DIM
    tiles = []
    for t in range(REC_BLOCKS // per_tile):
        tiles.append(jax.scipy.linalg.block_diag(*[w[t * per_tile + n] for n in range(per_tile)]))
    return jnp.stack(tiles).astype(BF16)


def _rec_ffn_kernel(x_ref, attn_ref, xr_ref, gate_ref,
                    cw_ref, cb_ref, wa_ref, wx_ref, ba_ref, bx_ref, lam_ref, rg_ref,
                    wo_ref, g1_ref, b1_ref, w1_ref, w2_ref, g2_ref, b2_ref,
                    o_ref, xprev_ref, hprev_ref, rec_ref, *, alpha, tiles_per_seq):
    s = pl.program_id(0)

    @pl.when(s % tiles_per_seq == 0)
    def _():
        xprev_ref[...] = jnp.zeros(xprev_ref.shape, F32)
        hprev_ref[...] = jnp.zeros(hprev_ref.shape, F32)

    t_base = (s % tiles_per_seq) * FFN_ROWS
    sub_tiles = [slice(r0, r0 + FFN_SUB_ROWS) for r0 in range(0, FFN_ROWS, FFN_SUB_ROWS)]
    n_chunks = D_FF // FFN_CHUNK

    def branch_pieces():
        z = -lam_ref[...]
        softplus = jnp.maximum(z, 0.0) + jnp.log1p(jnp.exp(-jnp.abs(z)))
        state = {"x_tail": xprev_ref[...], "h": hprev_ref[0:1, :]}
        convs = {}
        gates = {}

        def conv_piece(rows):
            def piece():
                x = xr_ref[rows, :]
                conv = _rec_conv(x, state["x_tail"], cw_ref[...], cb_ref[...])
                convs[rows.start] = (conv, conv.astype(BF16))
                state["x_tail"] = x[FFN_SUB_ROWS - SUBLANES:, :]
            return piece

        def gate_piece(rows):
            def piece():
                conv, conv_bf16 = convs[rows.start]
                gates[rows.start] = (conv,) + _rec_gate_dots(conv_bf16, wa_ref, wx_ref)
            return piece

        def scan_piece(rows, r0):
            def piece():
                conv, ra_pre, gi_pre = gates[rows.start]
                local = slice(r0 - rows.start, r0 - rows.start + REC_PIECE_ROWS)
                out, state["h"] = _rec_scan_rows(
                    conv[local, :], ra_pre[local, :], gi_pre[local, :],
                    gate_ref[r0:r0 + REC_PIECE_ROWS, :], state["h"], t_base + r0,
                    ba_ref[...], bx_ref[...], softplus, rg_ref[...])
                rec_ref[r0:r0 + REC_PIECE_ROWS, :] = out.astype(rec_ref.dtype)
            return piece

        def save_state():
            xprev_ref[...] = state["x_tail"]
            hprev_ref[0:1, :] = state["h"]

        pieces = []
        for rows in sub_tiles:
            pieces += [conv_piece(rows), gate_piece(rows)]
            pieces += [scan_piece(rows, r0)
                       for r0 in range(rows.start, rows.stop, REC_PIECE_ROWS)]
        return pieces, save_state

    def project(rows):
        mix = jnp.dot(attn_ref[rows, :], wo_ref[:ATTN_WIDTH, :], preferred_element_type=F32)
        return mix + jnp.dot(rec_ref[rows, :], wo_ref[ATTN_WIDTH:, :],
                             preferred_element_type=F32)

    def ffn_in(x1b, c):
        cols = slice(c * FFN_CHUNK, (c + 1) * FFN_CHUNK)
        ucols = slice(D_FF + c * FFN_CHUNK, D_FF + (c + 1) * FFN_CHUNK)
        return (jnp.dot(x1b, w1_ref[:, cols], preferred_element_type=F32),
                jnp.dot(x1b, w1_ref[:, ucols], preferred_element_type=F32))

    def ffn_out(gu, c):
        g, u = gu
        act = (g * jax.nn.sigmoid(g) * u).astype(BF16)
        return jnp.dot(act, w2_ref[c * FFN_CHUNK:(c + 1) * FFN_CHUNK, :],
                       preferred_element_type=F32)

    def first_norm(rows, mix):
        x1 = _layer_norm(alpha * x_ref[rows, :] + mix, g1_ref[...], b1_ref[...])
        return x1, x1.astype(BF16)

    @pl.when(s == 0)
    def _():
        pieces, save_state = branch_pieces()
        for piece in pieces:
            piece()
        save_state()

    @pl.when(s > 0)
    def _():
        pieces, save_state = branch_pieces()
        per_sub_tile = len(pieces) // len(sub_tiles)
        x1 = {0: first_norm(sub_tiles[0], project(sub_tiles[0]))}
        jobs = [(n, c) for n in range(len(sub_tiles)) for c in range(n_chunks)]
        pending = {0: ffn_in(x1[0][1], 0)}
        acc = {}
        mix = {}
        for j, (n, c) in enumerate(jobs):
            if j + 1 < len(jobs):
                n2, c2 = jobs[j + 1]
                pending[j + 1] = ffn_in(x1[n2][1], c2)
            part = ffn_out(pending.pop(j), c)
            acc[n] = part if c == 0 else acc[n] + part
            if c < per_sub_tile:
                pieces[n * per_sub_tile + c]()
            if n + 1 < len(sub_tiles) and c == n_chunks - 4:
                mix[n + 1] = project(sub_tiles[n + 1])
            if n + 1 < len(sub_tiles) and c == n_chunks - 3:
                x1[n + 1] = first_norm(sub_tiles[n + 1], mix.pop(n + 1))
            if c == n_chunks - 1:
                o_ref[sub_tiles[n], :] = _layer_norm(alpha * x1[n][0] + acc.pop(n),
                                                     g2_ref[...], b2_ref[...])
        save_state()


def _rec_ffn(x2d, attn2d, rec2d, seq, conv_w, conv_b, w_a, b_a, w_x, b_x, lru_lam, rec_g,
             w_out, ln1_g, ln1_b, w_ffn_in, w_ffn_out, ln2_g, ln2_b, alpha):
    rows = x2d.shape[0]
    tm = FFN_ROWS
    n_tiles = rows // tm
    row_vec = lambda p: p.reshape(1, -1)
    const = lambda shape: pl.BlockSpec(shape, lambda s: (0,) * len(shape),
                                       pipeline_mode=pl.Buffered(1))
    ffn_tile = lambda s: (jnp.maximum(s - 1, 0), 0)
    rec_tile = lambda s: jnp.minimum(s, n_tiles - 1)
    assert w_out.dtype == w_ffn_in.dtype == w_ffn_out.dtype == BF16
    return pl.pallas_call(
        functools.partial(_rec_ffn_kernel, alpha=alpha, tiles_per_seq=seq // tm),
        grid=(n_tiles + 1,),
        in_specs=[
            pl.BlockSpec((tm, D_MODEL), ffn_tile),
            pl.BlockSpec((tm, ATTN_WIDTH), ffn_tile),
            pl.BlockSpec((tm, REC_WIDTH), lambda s: (rec_tile(s), 0)),
            pl.BlockSpec((tm, REC_WIDTH), lambda s: (rec_tile(s), 1)),
            const((CONV_WIDTH, REC_WIDTH)), const((1, REC_WIDTH)),
            const((2, MXU_DIM, MXU_DIM)), const((2, MXU_DIM, MXU_DIM)),
            const((1, REC_WIDTH)), const((1, REC_WIDTH)), const((1, REC_WIDTH)),
            const((1, REC_WIDTH)),
            const((ATTN_WIDTH + REC_WIDTH, D_MODEL)),
            const((1, D_MODEL)), const((1, D_MODEL)),
            const((D_MODEL, 2 * D_FF)), const((D_FF, D_MODEL)),
            const((1, D_MODEL)), const((1, D_MODEL)),
        ],
        out_specs=pl.BlockSpec((tm, D_MODEL), ffn_tile),
        out_shape=jax.ShapeDtypeStruct((rows, D_MODEL), F32),
        scratch_shapes=[
            pltpu.VMEM((SUBLANES, REC_WIDTH), F32),
            pltpu.VMEM((SUBLANES, REC_WIDTH), F32),
            pltpu.VMEM((tm, REC_WIDTH), BF16),
        ],
        compiler_params=pltpu.CompilerParams(
            dimension_semantics=("arbitrary",), vmem_limit_bytes=VMEM_LIMIT_BYTES),
        name="rec_ffn",
    )(x2d, attn2d, rec2d, rec2d,
      conv_w, row_vec(conv_b), _block_diag_halves(w_a), _block_diag_halves(w_x),
      row_vec(b_a), row_vec(b_x), row_vec(lru_lam), row_vec(rec_g),
      w_out, row_vec(ln1_g), row_vec(ln1_b), w_ffn_in, w_ffn_out, row_vec(ln2_g), row_vec(ln2_b))


def kernel(x, w_in, conv_w, conv_b, lru_w_a, lru_b_a, lru_w_x, lru_b_x, lru_lambda,
           rec_norm_g, da_lambda, da_norm_g, w_out, ln1_g, ln1_b, w_ffn_in, w_ffn_out,
           ln2_g, ln2_b):
    batch, seq, _ = x.shape
    depth = w_in.shape[0]
    alpha = (2.0 * depth) ** 0.25
    x2d = x.reshape(batch * seq, D_MODEL)
    for l in range(depth):
        lambda_init = 0.8 - 0.6 * math.exp(-0.3 * l)
        qkv, rec, (w_out_b, w_ffn_in_b, w_ffn_out_b) = _in_proj(
            x2d, w_in[l].astype(BF16), (w_out[l], w_ffn_in[l], w_ffn_out[l]))
        attn = _diff_attention(qkv.reshape(batch, seq, QKV_WIDTH), da_lambda[l], da_norm_g[l],
                               lambda_init)
        x2d = _rec_ffn(x2d, attn.reshape(batch * seq, ATTN_WIDTH), rec, seq,
                       conv_w[l], conv_b[l], lru_w_a[l], lru_b_a[l], lru_w_x[l], lru_b_x[l],
                       lru_lambda[l], rec_norm_g[l], w_out_b, ln1_g[l], ln1_b[l],
                       w_ffn_in_b, w_ffn_out_b, ln2_g[l], ln2_b[l], alpha)
    return x2d.reshape(batch, seq, D_MODEL)
```

```python
import functools
import math

import jax
import jax.numpy as jnp
from jax import lax
from jax.experimental import pallas as pl
from jax.experimental.pallas import tpu as pltpu

D_MODEL = 1024
ATTN_WIDTH = 512
REC_WIDTH = 512
DA_HEAD_DIM = 64
DA_V_DIM = 128
DA_N_HEADS = 4
REC_BLOCKS = 8
REC_BLOCK_DIM = 64
CONV_WIDTH = 4
LRU_C = 8.0
D_FF = 2816
QKV_WIDTH = 3 * ATTN_WIDTH
LN_EPS = 1e-5
RMS_EPS = 1e-5

SUBLANES = 8
MXU_DIM = 256
VMEM_LIMIT_BYTES = 56 * 1024 * 1024

PROJ_ROWS = 512
ATTN_BLOCK = 256
ONES_ROWS = 16
ATTN_HEADS_PER_STEP = 2
ATTN_SCORE_LOOKAHEAD = 4
FFN_ROWS = 512
FFN_SUB_ROWS = 256
FFN_CHUNK = 256
REC_PIECE_ROWS = 64

NEG_BIG = -1e30
LOG2_E = 1.0 / math.log(2.0)
BF16 = jnp.bfloat16
F32 = jnp.float32


def _layer_norm(z, g, b):
    mu = jnp.mean(z, axis=-1, keepdims=True)
    zc = z - mu
    var = jnp.mean(zc * zc, axis=-1, keepdims=True)
    return zc * lax.rsqrt(var + LN_EPS) * g + b


def _in_proj_kernel(x_ref, w_ref, *refs):
    n_cast = (len(refs) - 2) // 2
    cast_in, (qkv_ref, rec_ref), cast_out = refs[:n_cast], refs[n_cast:n_cast + 2], refs[n_cast + 2:]
    xb = x_ref[...].astype(BF16)
    p = jnp.dot(xb, w_ref[...], preferred_element_type=F32)
    q = p[:, :ATTN_WIDTH] * (DA_HEAD_DIM ** -0.5 * LOG2_E)
    qkv_ref[:, :ATTN_WIDTH] = q.astype(BF16)
    qkv_ref[:, ATTN_WIDTH:] = p[:, ATTN_WIDTH:QKV_WIDTH].astype(BF16)
    rec_ref[...] = p[:, QKV_WIDTH:]
    for src_ref, dst_ref in zip(cast_in, cast_out):
        dst_ref[...] = src_ref[...].astype(dst_ref.dtype)


def _slab_rows(rows, n_steps):
    tile = 2 * SUBLANES
    slab = -(-rows // (n_steps * tile)) * tile
    while rows % slab:
        slab += tile
    return slab


def _in_proj(x2d, w_in_bf16, later_weights, layer):
    rows = x2d.shape[0]
    in_width = w_in_bf16.shape[1]
    n_steps = rows // PROJ_ROWS
    slab_in, slab_out, cast_shapes = [], [], []
    for w in later_weights:
        _, w_rows, w_cols = w.shape
        slab = _slab_rows(w_rows, n_steps)
        last = w_rows // slab - 1
        slab_in.append(pl.BlockSpec((None, slab, w_cols),
                                    lambda i, last=last: (layer, jnp.minimum(i, last), 0)))
        slab_out.append(pl.BlockSpec((slab, w_cols),
                                     lambda i, last=last: (jnp.minimum(i, last), 0)))
        cast_shapes.append(jax.ShapeDtypeStruct((w_rows, w_cols), BF16))
    outs = pl.pallas_call(
        _in_proj_kernel,
        grid=(n_steps,),
        in_specs=[
            pl.BlockSpec((PROJ_ROWS, D_MODEL), lambda i: (i, 0)),
            pl.BlockSpec((D_MODEL, in_width), lambda i: (0, 0)),
        ] + slab_in,
        out_specs=[
            pl.BlockSpec((PROJ_ROWS, QKV_WIDTH), lambda i: (i, 0)),
            pl.BlockSpec((PROJ_ROWS, 2 * REC_WIDTH), lambda i: (i, 0)),
        ] + slab_out,
        out_shape=[
            jax.ShapeDtypeStruct((rows, QKV_WIDTH), BF16),
            jax.ShapeDtypeStruct((rows, 2 * REC_WIDTH), F32),
        ] + cast_shapes,
        compiler_params=pltpu.CompilerParams(
            dimension_semantics=("arbitrary",), vmem_limit_bytes=VMEM_LIMIT_BYTES),
        name="in_proj",
    )(x2d, w_in_bf16, *later_weights)
    return outs[0], outs[1], outs[2:]


def _attn_kernel(lam_ref, g_ref, q_ref, k_ref, v_ref, o_ref, ka_ref, kb_ref, vext_ref,
                 *, seq, lambda_init):
    blk = ATTN_BLOCK
    lane = lax.broadcasted_iota(jnp.int32, (1, DA_V_DIM), 1)
    lo = (lane < DA_HEAD_DIM).astype(BF16)
    for hh in range(ATTN_HEADS_PER_STEP):
        cols = slice(hh * DA_V_DIM, (hh + 1) * DA_V_DIM)
        k_all = k_ref[:, cols]
        ka_ref[hh] = k_all * lo
        kb_ref[hh] = k_all * (1 - lo)
        vext_ref[hh, :DA_V_DIM, :] = v_ref[:, cols].astype(F32).T.astype(BF16)
        vext_ref[hh, DA_V_DIM:, :] = jnp.ones((ONES_ROWS, seq), BF16)

    lp = lam_ref[...]
    t1 = jnp.sum(lp[0:1, :] * lp[1:2, :], axis=-1, keepdims=True)
    t2 = jnp.sum(lp[2:3, :] * lp[3:4, :], axis=-1, keepdims=True)
    lam = jnp.exp(t1) - jnp.exp(t2) + lambda_init

    kv_idx = lax.broadcasted_iota(jnp.int32, (blk, blk), 0)
    q_idx = lax.broadcasted_iota(jnp.int32, (blk, blk), 1)
    causal = kv_idx <= q_idx
    dims = (((1,), (1,)), ((), ()))

    def scores(hh, i, kx_ref):
        kv_len = (i + 1) * blk
        q = q_ref[i * blk:kv_len, hh * DA_V_DIM:(hh + 1) * DA_V_DIM]
        s = lax.dot_general(kx_ref[hh, :kv_len, :], q, dims, preferred_element_type=F32)
        s_diag = jnp.where(causal, s[kv_len - blk:, :], NEG_BIG)
        if kv_len > blk:
            s = jnp.concatenate([s[:kv_len - blk, :], s_diag], axis=0)
        else:
            s = s_diag
        return s

    def softmax_numerator(s):
        m = jnp.max(s, axis=0, keepdims=True)
        return jnp.exp2(s - m).astype(BF16)

    def weighted_values(hh, i, p):
        kv_len = (i + 1) * blk
        r = jnp.dot(vext_ref[hh, :, :kv_len], p, preferred_element_type=F32)
        return r[:DA_V_DIM, :] / r[DA_V_DIM:DA_V_DIM + 1, :]

    n_blocks = seq // blk
    chains = [(hh, i, kx_ref) for i in reversed(range(n_blocks))
              for hh in range(ATTN_HEADS_PER_STEP) for kx_ref in (ka_ref, kb_ref)]
    n_chains = len(chains)
    ahead = ATTN_SCORE_LOOKAHEAD
    s_vals = {c: scores(*chains[c]) for c in range(ahead)}
    p_vals = {0: softmax_numerator(s_vals.pop(0))}
    outs = []
    for c, (hh, i, _) in enumerate(chains):
        if c + ahead < n_chains:
            s_vals[c + ahead] = scores(*chains[c + ahead])
        if c + 1 < n_chains:
            p_vals[c + 1] = softmax_numerator(s_vals.pop(c + 1))
        outs.append(weighted_values(hh, i, p_vals.pop(c)))
        if len(outs) == 2:
            o_t = outs[0] - lam * outs[1]
            outs = []
            ms = jnp.mean(o_t * o_t, axis=0, keepdims=True)
            o = (o_t * lax.rsqrt(ms + RMS_EPS)).T
            o = o * g_ref[...] * (1.0 - lambda_init)
            o_ref[i * blk:(i + 1) * blk, hh * DA_V_DIM:(hh + 1) * DA_V_DIM] = o.astype(o_ref.dtype)


def _diff_attention(qkv, da_lam, da_g, lambda_init):
    batch, seq, _ = qkv.shape
    kernel = functools.partial(_attn_kernel, seq=seq, lambda_init=lambda_init)
    hp = ATTN_HEADS_PER_STEP
    width = hp * DA_V_DIM
    groups = ATTN_WIDTH // width
    return pl.pallas_call(
        kernel,
        grid=(batch, groups),
        in_specs=[
            pl.BlockSpec((4, DA_HEAD_DIM), lambda b, h: (0, 0)),
            pl.BlockSpec((1, DA_V_DIM), lambda b, h: (0, 0)),
            pl.BlockSpec((None, seq, width), lambda b, h: (b, 0, h)),
            pl.BlockSpec((None, seq, width), lambda b, h: (b, 0, groups + h)),
            pl.BlockSpec((None, seq, width), lambda b, h: (b, 0, 2 * groups + h)),
        ],
        out_specs=pl.BlockSpec((None, seq, width), lambda b, h: (b, 0, h)),
        out_shape=jax.ShapeDtypeStruct((batch, seq, ATTN_WIDTH), BF16),
        scratch_shapes=[
            pltpu.VMEM((hp, seq, DA_V_DIM), BF16),
            pltpu.VMEM((hp, seq, DA_V_DIM), BF16),
            pltpu.VMEM((hp, DA_V_DIM + ONES_ROWS, seq), BF16),
        ],
        compiler_params=pltpu.CompilerParams(
            dimension_semantics=("arbitrary", "arbitrary"), vmem_limit_bytes=VMEM_LIMIT_BYTES),
        name="diff_attention",
    )(da_lam, da_g.reshape(1, DA_V_DIM), qkv, qkv, qkv)


def _gelu_tanh(x):
    c = math.sqrt(2.0 / math.pi)
    return x * (0.5 * (1.0 + jnp.tanh(c * (x + 0.044715 * (x * x * x)))))


def _rec_conv(x, x_tail, cw, cb):
    rows, width = x.shape
    tiles = rows // SUBLANES
    sub = lax.broadcasted_iota(jnp.int32, (1, SUBLANES, 1), 1)
    x3 = x.reshape(tiles, SUBLANES, width)
    conv = cb + x3 * cw[CONV_WIDTH - 1:CONV_WIDTH, :]
    for k in range(1, CONV_WIDTH):
        rot = pltpu.roll(x3, k, axis=1)
        rot_prev = jnp.concatenate([pltpu.roll(x_tail, k, axis=0)[None], rot[:-1]], axis=0)
        tap = CONV_WIDTH - 1 - k
        conv = conv + jnp.where(sub >= k, rot, rot_prev) * cw[tap:tap + 1, :]
    return conv.reshape(rows, width)


def _rec_gate_dots(conv_bf16, wa_ref, wx_ref):
    half = MXU_DIM

    def block_diag_dot(w_ref):
        lo = jnp.dot(conv_bf16[:, :half], w_ref[0], preferred_element_type=F32)
        hi = jnp.dot(conv_bf16[:, half:], w_ref[1], preferred_element_type=F32)
        return jnp.concatenate([lo, hi], axis=1)

    return block_diag_dot(wa_ref), block_diag_dot(wx_ref)


def _rec_scan_rows(conv, ra_pre, gi_pre, gate, h_last, t0, ba, bx, softplus, g):
    rows, width = conv.shape
    tiles = rows // SUBLANES
    sub = lax.broadcasted_iota(jnp.int32, (1, SUBLANES, 1), 1)
    r = jax.nn.sigmoid(ra_pre + ba)
    gi = jax.nn.sigmoid(gi_pre + bx)
    log_a = -LRU_C * r * softplus
    a = jnp.exp(log_a)
    mult = jnp.sqrt(-jnp.tanh(log_a) * (1.0 + a * a))
    row = lax.broadcasted_iota(jnp.int32, (rows, 1), 0)
    mult = jnp.where(row + t0 == 0, 1.0, mult)
    bt = mult * (gi * conv)

    a3 = a.reshape(tiles, SUBLANES, width)
    b3 = bt.reshape(tiles, SUBLANES, width)
    shift = 1
    while shift < SUBLANES:
        valid = sub >= shift
        a_sh = jnp.where(valid, pltpu.roll(a3, shift, axis=1), 1.0)
        b_sh = jnp.where(valid, pltpu.roll(b3, shift, axis=1), 0.0)
        b3 = a3 * b_sh + b3
        a3 = a3 * a_sh
        shift *= 2
    carry = h_last
    h_tiles = []
    for j in range(tiles):
        h_j = a3[j] * carry + b3[j]
        h_tiles.append(h_j)
        carry = h_j[SUBLANES - 1:SUBLANES, :]
    h = jnp.concatenate(h_tiles, axis=0)

    y = h * _gelu_tanh(gate)
    ms = jnp.mean(y * y, axis=-1, keepdims=True)
    return y * lax.rsqrt(ms + RMS_EPS) * g, carry


def _block_diag_halves(w):
    per_tile = MXU_DIM // REC_BLOCK_DIM
    tiles = []
    for t in range(REC_BLOCKS // per_tile):
        tiles.append(jax.scipy.linalg.block_diag(*[w[t * per_tile + n] for n in range(per_tile)]))
    return jnp.stack(tiles).astype(BF16)


def _rec_ffn_kernel(x_ref, attn_ref, xr_ref, gate_ref,
                    cw_ref, cb_ref, wa_ref, wx_ref, ba_ref, bx_ref, lam_ref, rg_ref,
                    wo_ref, g1_ref, b1_ref, w1_ref, w2_ref, g2_ref, b2_ref,
                    o_ref, xprev_ref, hprev_ref, rec_ref, *, alpha, tiles_per_seq):
    s = pl.program_id(0)

    @pl.when(s % tiles_per_seq == 0)
    def _():
        xprev_ref[...] = jnp.zeros(xprev_ref.shape, F32)
        hprev_ref[...] = jnp.zeros(hprev_ref.shape, F32)

    t_base = (s % tiles_per_seq) * FFN_ROWS
    sub_tiles = [slice(r0, r0 + FFN_SUB_ROWS) for r0 in range(0, FFN_ROWS, FFN_SUB_ROWS)]
    n_chunks = D_FF // FFN_CHUNK

    def branch_pieces():
        z = -lam_ref[...]
        softplus = jnp.maximum(z, 0.0) + jnp.log1p(jnp.exp(-jnp.abs(z)))
        state = {"x_tail": xprev_ref[...], "h": hprev_ref[0:1, :]}
        convs = {}
        gates = {}

        def conv_piece(rows):
            def piece():
                x = xr_ref[rows, :]
                conv = _rec_conv(x, state["x_tail"], cw_ref[...], cb_ref[...])
                convs[rows.start] = (conv, conv.astype(BF16))
                state["x_tail"] = x[FFN_SUB_ROWS - SUBLANES:, :]
            return piece

        def gate_piece(rows):
            def piece():
                conv, conv_bf16 = convs[rows.start]
                gates[rows.start] = (conv,) + _rec_gate_dots(conv_bf16, wa_ref, wx_ref)
            return piece

        def scan_piece(rows, r0):
            def piece():
                conv, ra_pre, gi_pre = gates[rows.start]
                local = slice(r0 - rows.start, r0 - rows.start + REC_PIECE_ROWS)
                out, state["h"] = _rec_scan_rows(
                    conv[local, :], ra_pre[local, :], gi_pre[local, :],
                    gate_ref[r0:r0 + REC_PIECE_ROWS, :], state["h"], t_base + r0,
                    ba_ref[...], bx_ref[...], softplus, rg_ref[...])
                rec_ref[r0:r0 + REC_PIECE_ROWS, :] = out.astype(rec_ref.dtype)
            return piece

        def save_state():
            xprev_ref[...] = state["x_tail"]
            hprev_ref[0:1, :] = state["h"]

        pieces = []
        for rows in sub_tiles:
            pieces += [conv_piece(rows), gate_piece(rows)]
            pieces += [scan_piece(rows, r0)
                       for r0 in range(rows.start, rows.stop, REC_PIECE_ROWS)]
        return pieces, save_state

    def project(rows):
        mix = jnp.dot(attn_ref[rows, :], wo_ref[:ATTN_WIDTH, :], preferred_element_type=F32)
        return mix + jnp.dot(rec_ref[rows, :], wo_ref[ATTN_WIDTH:, :],
                             preferred_element_type=F32)

    def ffn_in(x1b, c):
        cols = slice(c * FFN_CHUNK, (c + 1) * FFN_CHUNK)
        ucols = slice(D_FF + c * FFN_CHUNK, D_FF + (c + 1) * FFN_CHUNK)
        return (jnp.dot(x1b, w1_ref[:, cols], preferred_element_type=F32),
                jnp.dot(x1b, w1_ref[:, ucols], preferred_element_type=F32))

    def ffn_out(gu, c):
        g, u = gu
        act = (g * jax.nn.sigmoid(g) * u).astype(BF16)
        return jnp.dot(act, w2_ref[c * FFN_CHUNK:(c + 1) * FFN_CHUNK, :],
                       preferred_element_type=F32)

    def first_norm(rows, mix):
        x1 = _layer_norm(alpha * x_ref[rows, :] + mix, g1_ref[...], b1_ref[...])
        return x1, x1.astype(BF16)

    @pl.when(s == 0)
    def _():
        pieces, save_state = branch_pieces()
        for piece in pieces:
            piece()
        save_state()

    @pl.when(s > 0)
    def _():
        pieces, save_state = branch_pieces()
        per_sub_tile = len(pieces) // len(sub_tiles)
        x1 = {0: first_norm(sub_tiles[0], project(sub_tiles[0]))}
        jobs = [(n, c) for n in range(len(sub_tiles)) for c in range(n_chunks)]
        pending = {0: ffn_in(x1[0][1], 0)}
        acc = {}
        mix = {}
        for j, (n, c) in enumerate(jobs):
            if j + 1 < len(jobs):
                n2, c2 = jobs[j + 1]
                pending[j + 1] = ffn_in(x1[n2][1], c2)
            part = ffn_out(pending.pop(j), c)
            acc[n] = part if c == 0 else acc[n] + part
            if c < per_sub_tile:
                pieces[n * per_sub_tile + c]()
            if n + 1 < len(sub_tiles) and c == n_chunks - 4:
                mix[n + 1] = project(sub_tiles[n + 1])
            if n + 1 < len(sub_tiles) and c == n_chunks - 3:
                x1[n + 1] = first_norm(sub_tiles[n + 1], mix.pop(n + 1))
            if c == n_chunks - 1:
                o_ref[sub_tiles[n], :] = _layer_norm(alpha * x1[n][0] + acc.pop(n),
                                                     g2_ref[...], b2_ref[...])
        save_state()


def _rec_ffn(x2d, attn2d, rec2d, seq, conv_w, conv_b, w_a, b_a, w_x, b_x, lru_lam, rec_g,
             w_out, ln1_g, ln1_b, w_ffn_in, w_ffn_out, ln2_g, ln2_b, alpha):
    rows = x2d.shape[0]
    tm = FFN_ROWS
    n_tiles = rows // tm
    row_vec = lambda p: p.reshape(1, -1)
    const = lambda shape: pl.BlockSpec(shape, lambda s: (0,) * len(shape),
                                       pipeline_mode=pl.Buffered(1))
    ffn_tile = lambda s: (jnp.maximum(s - 1, 0), 0)
    rec_tile = lambda s: jnp.minimum(s, n_tiles - 1)
    assert w_out.dtype == w_ffn_in.dtype == w_ffn_out.dtype == BF16
    return pl.pallas_call(
        functools.partial(_rec_ffn_kernel, alpha=alpha, tiles_per_seq=seq // tm),
        grid=(n_tiles + 1,),
        in_specs=[
            pl.BlockSpec((tm, D_MODEL), ffn_tile),
            pl.BlockSpec((tm, ATTN_WIDTH), ffn_tile),
            pl.BlockSpec((tm, REC_WIDTH), lambda s: (rec_tile(s), 0)),
            pl.BlockSpec((tm, REC_WIDTH), lambda s: (rec_tile(s), 1)),
            const((CONV_WIDTH, REC_WIDTH)), const((1, REC_WIDTH)),
            const((2, MXU_DIM, MXU_DIM)), const((2, MXU_DIM, MXU_DIM)),
            const((1, REC_WIDTH)), const((1, REC_WIDTH)), const((1, REC_WIDTH)),
            const((1, REC_WIDTH)),
            const((ATTN_WIDTH + REC_WIDTH, D_MODEL)),
            const((1, D_MODEL)), const((1, D_MODEL)),
            const((D_MODEL, 2 * D_FF)), const((D_FF, D_MODEL)),
            const((1, D_MODEL)), const((1, D_MODEL)),
        ],
        out_specs=pl.BlockSpec((tm, D_MODEL), ffn_tile),
        out_shape=jax.ShapeDtypeStruct((rows, D_MODEL), F32),
        scratch_shapes=[
            pltpu.VMEM((SUBLANES, REC_WIDTH), F32),
            pltpu.VMEM((SUBLANES, REC_WIDTH), F32),
            pltpu.VMEM((tm, REC_WIDTH), BF16),
        ],
        compiler_params=pltpu.CompilerParams(
            dimension_semantics=("arbitrary",), vmem_limit_bytes=VMEM_LIMIT_BYTES),
        name="rec_ffn",
    )(x2d, attn2d, rec2d, rec2d,
      conv_w, row_vec(conv_b), _block_diag_halves(w_a), _block_diag_halves(w_x),
      row_vec(b_a), row_vec(b_x), row_vec(lru_lam), row_vec(rec_g),
      w_out, row_vec(ln1_g), row_vec(ln1_b), w_ffn_in, w_ffn_out, row_vec(ln2_g), row_vec(ln2_b))


def kernel(x, w_in, conv_w, conv_b, lru_w_a, lru_b_a, lru_w_x, lru_b_x, lru_lambda,
           rec_norm_g, da_lambda, da_norm_g, w_out, ln1_g, ln1_b, w_ffn_in, w_ffn_out,
           ln2_g, ln2_b):
    batch, seq, _ = x.shape
    depth = w_in.shape[0]
    alpha = (2.0 * depth) ** 0.25
    x2d = x.reshape(batch * seq, D_MODEL)
    for l in range(depth):
        lambda_init = 0.8 - 0.6 * math.exp(-0.3 * l)
        qkv, rec, (w_out_b, w_ffn_in_b, w_ffn_out_b) = _in_proj(
            x2d, w_in[l].astype(BF16), (w_out, w_ffn_in, w_ffn_out), l)
        attn = _diff_attention(qkv.reshape(batch, seq, QKV_WIDTH), da_lambda[l], da_norm_g[l],
                               lambda_init)
        x2d = _rec_ffn(x2d, attn.reshape(batch * seq, ATTN_WIDTH), rec, seq,
                       conv_w[l], conv_b[l], lru_w_a[l], lru_b_a[l], lru_w_x[l], lru_b_x[l],
                       lru_lambda[l], rec_norm_g[l], w_out_b, ln1_g[l], ln1_b[l],
                       w_ffn_in_b, w_ffn_out_b, ln2_g[l], ln2_b[l], alpha)
    return x2d.reshape(batch, seq, D_MODEL)
```
